```python
import math
import jax, jax.numpy as jnp
from jax import lax
import numpy as np

D_MODEL = 1024
BATCH = 1
SEQ = 16384
DEPTH = 2
DEC_BATCH = 32
DEC_SEQ = 32
PAST_LEN = 1024

CHUNK = 64
N_MIXERS = 2
N_A_LAYERS = (DEPTH + 1) // 2
N_B_LAYERS = DEPTH // 2

MLA_HEADS = 16
MLA_Q_LORA = 384
MLA_KV_LORA = 256
MLA_NOPE = 64
MLA_ROPE = 32
MLA_QK = MLA_NOPE + MLA_ROPE
MLA_V = 64
ROPE_THETA = 10000.0
Q_BLOCK = 128

BAND_HEADS = 16
BAND_HEAD_DIM = D_MODEL // BAND_HEADS
LEFT_CHUNKS = 8
BAND_CHUNKS = LEFT_CHUNKS + 1
BAND_KEEP_MAX = LEFT_CHUNKS * CHUNK
REL_MAX = 128

N_EXPERTS = 16
N_GROUPS = 4
EXPERTS_PER_GROUP = N_EXPERTS // N_GROUPS
TOP_K = 2
D_EXPERT = 256

DEEPNORM_ALPHA = (2.0 * DEPTH) ** 0.25
DEEPNORM_BETA = (8.0 * DEPTH) ** -0.25
NORM_EPS = 1e-5
NEG_INF = -1e30

kernel_name = 'hybrid_mla_chunkband_grouped_moe_stream_step'


def layer_norm(x, g, b):
    xf = x.astype(jnp.float32)
    mu = jnp.mean(xf, -1, keepdims=True)
    var = jnp.mean(jnp.square(xf - mu), -1, keepdims=True)
    return ((xf - mu) * lax.rsqrt(var + NORM_EPS) * g + b).astype(x.dtype)


def rms_norm(x, g):
    xf = x.astype(jnp.float32)
    return (xf * lax.rsqrt(jnp.mean(jnp.square(xf), -1, keepdims=True) + NORM_EPS) * g).astype(x.dtype)


def rope(x, pos):
    half = MLA_ROPE // 2
    inv = ROPE_THETA ** (-jnp.arange(half, dtype=jnp.float32) / half)
    ang = pos.astype(jnp.float32)[:, None] * inv[None, :]
    cos = jnp.cos(ang)[:, None, :]
    sin = jnp.sin(ang)[:, None, :]
    x1 = x[..., :half].astype(jnp.float32)
    x2 = x[..., half:].astype(jnp.float32)
    return jnp.concatenate([x1 * cos - x2 * sin, x1 * sin + x2 * cos], -1).astype(x.dtype)


def mla_project(x, pos, w_in, q_norm, kv_norm, w_uq):
    B, S, _ = x.shape
    a = x @ w_in
    c_q = rms_norm(a[..., :MLA_Q_LORA], q_norm)
    c_kv = rms_norm(a[..., MLA_Q_LORA:MLA_Q_LORA + MLA_KV_LORA], kv_norm)
    k_rope = rope(a[..., MLA_Q_LORA + MLA_KV_LORA:][:, :, None, :], pos)[:, :, 0, :]
    q = (c_q @ w_uq).reshape(B, S, MLA_HEADS, MLA_QK)
    q = jnp.concatenate([q[..., :MLA_NOPE], rope(q[..., MLA_NOPE:], pos)], -1)
    return q, c_kv, k_rope


def mla_expand(c_kv, k_rope, w_uk, w_uv):
    B, S, _ = c_kv.shape
    k_nope = (c_kv @ w_uk).reshape(B, S, MLA_HEADS, MLA_NOPE)
    k = jnp.concatenate([k_nope, jnp.broadcast_to(k_rope[:, :, None, :], (B, S, MLA_HEADS, MLA_ROPE))], -1)
    v = (c_kv @ w_uv).reshape(B, S, MLA_HEADS, MLA_V)
    return k, v


def chunk_causal_attend(q, k, v, q_pos, k_pos):
    s = jnp.einsum('bqhd,bkhd->bhqk', q, k).astype(jnp.float32) * (1.0 / math.sqrt(MLA_QK))
    mask = (k_pos // CHUNK)[None, :] <= (q_pos // CHUNK)[:, None]
    p = jax.nn.softmax(jnp.where(mask, s, NEG_INF), axis=-1).astype(v.dtype)
    return jnp.einsum('bhqk,bkhd->bqhd', p, v)


def mla_prompt(x, w_in, q_norm, kv_norm, w_uq, w_uk, w_uv, w_o):
    B, S, _ = x.shape
    pos = jnp.arange(S, dtype=jnp.int32)
    q, c_kv, k_rope = mla_project(x, pos, w_in, q_norm, kv_norm, w_uq)
    k, v = mla_expand(c_kv, k_rope, w_uk, w_uv)
    nb = S // Q_BLOCK
    q_blocks = jnp.moveaxis(q.reshape(B, nb, Q_BLOCK, MLA_HEADS, MLA_QK), 1, 0)
    p_blocks = pos.reshape(nb, Q_BLOCK)
    o = lax.map(lambda qp: chunk_causal_attend(qp[0], k, v, qp[1], pos), (q_blocks, p_blocks))
    o = jnp.moveaxis(o, 0, 1).reshape(B, S, MLA_HEADS * MLA_V)
    return o @ w_o, c_kv, k_rope


def mla_sample(x, cache_ckv, cache_krope, w_in, q_norm, kv_norm, w_uq, w_uk, w_uv, w_o):
    B, T, _ = x.shape
    q_pos = PAST_LEN + jnp.arange(T, dtype=jnp.int32)
    q, c_kv, k_rope = mla_project(x, q_pos, w_in, q_norm, kv_norm, w_uq)
    k, v = mla_expand(jnp.concatenate([cache_ckv, c_kv], 1), jnp.concatenate([cache_krope, k_rope], 1), w_uk, w_uv)
    k_pos = jnp.arange(PAST_LEN + T, dtype=jnp.int32)
    o = chunk_causal_attend(q, k, v, q_pos, k_pos).reshape(B, T, MLA_HEADS * MLA_V)
    return o @ w_o, c_kv, k_rope


def band_qkv(x, w_qkv):
    B, S, _ = x.shape
    qkv = (x @ w_qkv).reshape(B, S, 3, BAND_HEADS, BAND_HEAD_DIM)
    return qkv[:, :, 0], qkv[:, :, 1], qkv[:, :, 2]


def band_attend(q, k, v, q_pos, k_pos, rel_table):
    s = jnp.einsum('bqhd,bkhd->bhqk', q, k).astype(jnp.float32) * (1.0 / math.sqrt(BAND_HEAD_DIM))
    rel = jnp.clip(q_pos[:, None] - k_pos[None, :], -REL_MAX, REL_MAX) + REL_MAX
    bias = rel_table.astype(jnp.float32)[:, rel]
    qc = (q_pos // CHUNK)[:, None]
    kc = (k_pos // CHUNK)[None, :]
    mask = (kc <= qc) & (kc >= qc - LEFT_CHUNKS) & (k_pos[None, :] >= 0)
    p = jax.nn.softmax(jnp.where(mask, s + bias[None], NEG_INF), axis=-1).astype(v.dtype)
    return jnp.einsum('bhqk,bkhd->bqhd', p, v)


def band_prompt(x, w_qkv, rel_table, w_o):
    B, S, _ = x.shape
    q, k, v = band_qkv(x, w_qkv)
    nc = S // CHUNK
    pad = ((0, 0), (BAND_KEEP_MAX, 0), (0, 0), (0, 0))
    kp = jnp.pad(k, pad)
    vp = jnp.pad(v, pad)
    q_chunks = jnp.moveaxis(q.reshape(B, nc, CHUNK, BAND_HEADS, BAND_HEAD_DIM), 1, 0)
    q_offs = jnp.arange(CHUNK, dtype=jnp.int32)
    k_offs = jnp.arange(BAND_CHUNKS * CHUNK, dtype=jnp.int32)

    def one_chunk(args):
        qc, c = args
        start = c * CHUNK
        kb = lax.dynamic_slice_in_dim(kp, start, BAND_CHUNKS * CHUNK, axis=1)
        vb = lax.dynamic_slice_in_dim(vp, start, BAND_CHUNKS * CHUNK, axis=1)
        return band_attend(qc, kb, vb, start + q_offs, start - BAND_KEEP_MAX + k_offs, rel_table)

    o = lax.map(one_chunk, (q_chunks, jnp.arange(nc, dtype=jnp.int32)))
    o = jnp.moveaxis(o, 0, 1).reshape(B, S, BAND_HEADS * BAND_HEAD_DIM)
    keep = min(BAND_KEEP_MAX, S)
    return o @ w_o, k[:, S - keep:], v[:, S - keep:]


def band_sample(x, cache_k, cache_v, w_qkv, rel_table, w_o):
    B, T, _ = x.shape
    keep = cache_k.shape[1]
    q, k, v = band_qkv(x, w_qkv)
    q_pos = PAST_LEN + jnp.arange(T, dtype=jnp.int32)
    k_pos = PAST_LEN - keep + jnp.arange(keep + T, dtype=jnp.int32)
    o = band_attend(q, jnp.concatenate([cache_k, k], 1), jnp.concatenate([cache_v, v], 1), q_pos, k_pos, rel_table)
    return o.reshape(B, T, BAND_HEADS * BAND_HEAD_DIM) @ w_o, k, v


def moe(x, router_w, router_b, w_gate, w_up, w_down):
    B, S, D = x.shape
    t = x.reshape(B * S, D)
    scores = jax.nn.sigmoid((t @ router_w).astype(jnp.float32))
    biased = scores + router_b.astype(jnp.float32)
    grp_top = lax.top_k(biased.reshape(-1, N_GROUPS, EXPERTS_PER_GROUP), TOP_K)[0]
    g_sel = jnp.argmax(jnp.sum(grp_top, -1), -1)
    in_group = (jnp.arange(N_EXPERTS) // EXPERTS_PER_GROUP)[None, :] == g_sel[:, None]
    _, idx = lax.top_k(jnp.where(in_group, biased, NEG_INF), TOP_K)
    w_sel = jnp.take_along_axis(scores, idx, -1)
    w_sel = w_sel / jnp.sum(w_sel, -1, keepdims=True)
    comb = jnp.sum(jax.nn.one_hot(idx, N_EXPERTS, dtype=jnp.float32) * w_sel[..., None], axis=1)
    h = jax.nn.silu(jnp.einsum('td,edf->tef', t, w_gate)) * jnp.einsum('td,edf->tef', t, w_up)
    h = h * comb[..., None].astype(h.dtype)
    return jnp.einsum('tef,efd->td', h, w_down).reshape(B, S, D)


def setup_inputs(seed: int = 0) -> dict:
    key = jax.random.key(seed)
    ks = jax.random.split(key, 32)
    f32 = jnp.float32
    nrm = lambda k, shape, scale: jax.random.normal(k, shape, f32) * scale
    band_keep = min(BAND_KEEP_MAX, PAST_LEN)
    return {
        'x_prompt': nrm(ks[0], (BATCH, SEQ, D_MODEL), 1.0),
        'x_sample': nrm(ks[1], (DEC_BATCH, DEC_SEQ, D_MODEL), 1.0),
        'cache_mla_ckv': nrm(ks[2], (N_A_LAYERS, DEC_BATCH, PAST_LEN, MLA_KV_LORA), 1.0),
        'cache_mla_krope': nrm(ks[3], (N_A_LAYERS, DEC_BATCH, PAST_LEN, MLA_ROPE), 1.0),
        'cache_band_k': nrm(ks[4], (N_B_LAYERS, DEC_BATCH, band_keep, BAND_HEADS, BAND_HEAD_DIM), 1.0),
        'cache_band_v': nrm(ks[5], (N_B_LAYERS, DEC_BATCH, band_keep, BAND_HEADS, BAND_HEAD_DIM), 1.0),
        'mla_w_in': nrm(ks[6], (N_A_LAYERS, D_MODEL, MLA_Q_LORA + MLA_KV_LORA + MLA_ROPE), D_MODEL ** -0.5),
        'mla_q_norm': 1.0 + nrm(ks[7], (N_A_LAYERS, MLA_Q_LORA), 0.01),
        'mla_kv_norm': 1.0 + nrm(ks[8], (N_A_LAYERS, MLA_KV_LORA), 0.01),
        'mla_w_uq': nrm(ks[9], (N_A_LAYERS, MLA_Q_LORA, MLA_HEADS * MLA_QK), MLA_Q_LORA ** -0.5),
        'mla_w_uk': nrm(ks[10], (N_A_LAYERS, MLA_KV_LORA, MLA_HEADS * MLA_NOPE), MLA_KV_LORA ** -0.5),
        'mla_w_uv': nrm(ks[11], (N_A_LAYERS, MLA_KV_LORA, MLA_HEADS * MLA_V), MLA_KV_LORA ** -0.5),
        'mla_w_o': nrm(ks[12], (N_A_LAYERS, MLA_HEADS * MLA_V, D_MODEL), (MLA_HEADS * MLA_V) ** -0.5 * DEEPNORM_BETA),
        'band_w_qkv': nrm(ks[13], (N_B_LAYERS, D_MODEL, 3 * BAND_HEADS * BAND_HEAD_DIM), D_MODEL ** -0.5),
        'band_rel_bias': nrm(ks[14], (N_B_LAYERS, BAND_HEADS, 2 * REL_MAX + 1), 0.5),
        'band_w_o': nrm(ks[15], (N_B_LAYERS, BAND_HEADS * BAND_HEAD_DIM, D_MODEL), (BAND_HEADS * BAND_HEAD_DIM) ** -0.5 * DEEPNORM_BETA),
        'router_w': nrm(ks[16], (D_MODEL, N_EXPERTS), D_MODEL ** -0.5),
        'router_b': nrm(ks[17], (N_EXPERTS,), 0.01),
        'moe_w_gate': nrm(ks[18], (DEPTH, N_EXPERTS, D_MODEL, D_EXPERT), D_MODEL ** -0.5),
        'moe_w_up': nrm(ks[19], (DEPTH, N_EXPERTS, D_MODEL, D_EXPERT), D_MODEL ** -0.5),
        'moe_w_down': nrm(ks[20], (DEPTH, N_EXPERTS, D_EXPERT, D_MODEL), D_EXPERT ** -0.5 * DEEPNORM_BETA),
        'ln_mix_g': 1.0 + nrm(ks[21], (DEPTH, D_MODEL), 0.01),
        'ln_mix_b': nrm(ks[22], (DEPTH, D_MODEL), 0.01),
        'ln_ffn_g': 1.0 + nrm(ks[23], (DEPTH, D_MODEL), 0.01),
        'ln_ffn_b': nrm(ks[24], (DEPTH, D_MODEL), 0.01),
    }


def reference(x_prompt, x_sample, cache_mla_ckv, cache_mla_krope, cache_band_k, cache_band_v,
              mla_w_in, mla_q_norm, mla_kv_norm, mla_w_uq, mla_w_uk, mla_w_uv, mla_w_o,
              band_w_qkv, band_rel_bias, band_w_o,
              router_w, router_b, moe_w_gate, moe_w_up, moe_w_down,
              ln_mix_g, ln_mix_b, ln_ffn_g, ln_ffn_b):
    xp, xs = x_prompt, x_sample
    ckv_p, kr_p, ckv_s, kr_s = [], [], [], []
    bk_p, bv_p, bk_s, bv_s = [], [], [], []
    for i in range(DEPTH):
        j = i // N_MIXERS
        if i % N_MIXERS == 0:
            w = (mla_w_in[j], mla_q_norm[j], mla_kv_norm[j], mla_w_uq[j], mla_w_uk[j], mla_w_uv[j], mla_w_o[j])
            mp, c_p, r_p = mla_prompt(xp, *w)
            ms, c_s, r_s = mla_sample(xs, cache_mla_ckv[j], cache_mla_krope[j], *w)
            ckv_p.append(c_p); kr_p.append(r_p); ckv_s.append(c_s); kr_s.append(r_s)
        else:
            mp, k_p, v_p = band_prompt(xp, band_w_qkv[j], band_rel_bias[j], band_w_o[j])
            ms, k_s, v_s = band_sample(xs, cache_band_k[j], cache_band_v[j], band_w_qkv[j], band_rel_bias[j], band_w_o[j])
            bk_p.append(k_p); bv_p.append(v_p); bk_s.append(k_s); bv_s.append(v_s)
        xp = layer_norm(DEEPNORM_ALPHA * xp + mp, ln_mix_g[i], ln_mix_b[i])
        xs = layer_norm(DEEPNORM_ALPHA * xs + ms, ln_mix_g[i], ln_mix_b[i])
        xp = layer_norm(DEEPNORM_ALPHA * xp + moe(xp, router_w, router_b, moe_w_gate[i], moe_w_up[i], moe_w_down[i]), ln_ffn_g[i], ln_ffn_b[i])
        xs = layer_norm(DEEPNORM_ALPHA * xs + moe(xs, router_w, router_b, moe_w_gate[i], moe_w_up[i], moe_w_down[i]), ln_ffn_g[i], ln_ffn_b[i])
    return (xp, xs,
            jnp.stack(ckv_p, 0), jnp.stack(kr_p, 0), jnp.stack(ckv_s, 0), jnp.stack(kr_s, 0),
            jnp.stack(bk_p, 0), jnp.stack(bv_p, 0), jnp.stack(bk_s, 0), jnp.stack(bv_s, 0))
```

```python
import functools
import math

import numpy as np
import jax
import jax.numpy as jnp
from jax import lax
from jax.experimental import pallas as pl
from jax.experimental.pallas import tpu as pltpu

D_MODEL = 1024
SEQ = 16384
DEPTH = 2
DEC_BATCH = 32
DEC_SEQ = 32
PAST_LEN = 1024
CHUNK = 64
N_MIXERS = 2

MLA_HEADS = 16
MLA_Q_LORA = 384
MLA_KV_LORA = 256
MLA_NOPE = 64
MLA_ROPE = 32
MLA_QK = MLA_NOPE + MLA_ROPE
MLA_V = 64
ROPE_THETA = 10000.0

BAND_HEADS = 16
BAND_HEAD_DIM = D_MODEL // BAND_HEADS
LEFT_CHUNKS = 8
BAND_KEEP_MAX = LEFT_CHUNKS * CHUNK
REL_MAX = 128

N_EXPERTS = 16
N_GROUPS = 4
EXPERTS_PER_GROUP = N_EXPERTS // N_GROUPS
D_EXPERT = 256

DEEPNORM_ALPHA = (2.0 * DEPTH) ** 0.25
NORM_EPS = 1e-5
NEG_INF = -1e30

N_SAMPLE = DEC_BATCH * DEC_SEQ
N_TOKENS = SEQ + N_SAMPLE
N_PAIRS = MLA_HEADS // 2
LANES = 128
VMEM_LIMIT = 56 * 1024 * 1024

TOKEN_TILE = 512
MOE_TILE = 512
ATTN_TQ = 512
ATTN_TK = 512
BAND_QSUB = 128
BAND_WIN = BAND_KEEP_MAX + BAND_QSUB
BAND_QBLOCK = 1024

BF16 = jnp.bfloat16
F32 = jnp.float32


def _params(*sem):
    return pltpu.CompilerParams(dimension_semantics=sem, vmem_limit_bytes=VMEM_LIMIT)


def _dot(a, b):
    return jnp.dot(a, b, preferred_element_type=F32)


def _dot_nt(a, b):
    return lax.dot_general(a, b, (((1,), (1,)), ((), ())), preferred_element_type=F32)


def _layer_norm(y, g, b):
    mu = jnp.mean(y, -1, keepdims=True)
    d = y - mu
    var = jnp.mean(d * d, -1, keepdims=True)
    return d * lax.rsqrt(var + NORM_EPS) * g + b


def _rms_norm(y, g):
    return y * lax.rsqrt(jnp.mean(y * y, -1, keepdims=True) + NORM_EPS) * g


def _mla_proj_kernel(x_ref, w_in_ref, qn_ref, kvn_ref, wqa_ref, wqb_ref, cos_ref, sin_ref,
                     ckv_ref, kr_ref, q_ref):
    xb = x_ref[...].astype(BF16)
    a = _dot(xb, w_in_ref[...])
    cq = _rms_norm(a[:, :MLA_Q_LORA], qn_ref[...])
    o_kv = MLA_Q_LORA + MLA_KV_LORA
    ckv_ref[...] = _rms_norm(a[:, MLA_Q_LORA:o_kv], kvn_ref[...])
    cos = cos_ref[...]
    sin = sin_ref[...]
    kr_ref[...] = a[:, o_kv:o_kv + LANES] * cos + a[:, o_kv + LANES:] * sin
    cqb = cq.astype(BF16)
    qa = _dot(cqb, wqa_ref[...])
    qb = _dot(cqb, wqb_ref[...])
    scale = 1.0 / math.sqrt(MLA_QK)
    for j in range(N_PAIRS):
        c0 = 2 * LANES * j
        q_ref[:, c0:c0 + LANES] = (qa[:, c0:c0 + LANES] * scale).astype(BF16)
        rot = qa[:, c0 + LANES:c0 + 2 * LANES] * cos + qb[:, LANES * j:LANES * (j + 1)] * sin
        q_ref[:, c0 + LANES:c0 + 2 * LANES] = (rot * scale).astype(BF16)


def _mla_proj(x, w_in_p, qn, kvn, wqa, wqb, cos, sin):
    n = x.shape[0]
    tm = TOKEN_TILE
    row = lambda w: pl.BlockSpec((tm, w), lambda i: (i, 0))
    full = lambda a: pl.BlockSpec(a.shape, lambda i: (0, 0))
    return pl.pallas_call(
        _mla_proj_kernel,
        grid=(n // tm,),
        in_specs=[row(D_MODEL), full(w_in_p), full(qn), full(kvn), full(wqa), full(wqb),
                  row(LANES), row(LANES)],
        out_specs=[row(MLA_KV_LORA), row(LANES), row(2 * LANES * N_PAIRS)],
        out_shape=[jax.ShapeDtypeStruct((n, MLA_KV_LORA), F32),
                   jax.ShapeDtypeStruct((n, LANES), F32),
                   jax.ShapeDtypeStruct((n, 2 * LANES * N_PAIRS), BF16)],
        compiler_params=_params("parallel"),
    )(x, w_in_p, qn, kvn, wqa, wqb, cos, sin)


def _kv_expand_kernel(ckv_ref, kr_ref, wuk_ref, wuv_ref, k_ref, v_ref):
    cb = ckv_ref[...].astype(BF16)
    kn = _dot(cb, wuk_ref[...])
    v_ref[...] = _dot(cb, wuv_ref[...]).astype(BF16)
    krb = kr_ref[...].astype(BF16)
    for j in range(N_PAIRS):
        c0 = 2 * LANES * j
        k_ref[:, c0:c0 + LANES] = kn[:, LANES * j:LANES * (j + 1)].astype(BF16)
        k_ref[:, c0 + LANES:c0 + 2 * LANES] = krb


def _kv_expand(ckv, kr128, wuk, wuv, n):
    tm = TOKEN_TILE
    row = lambda w: pl.BlockSpec((tm, w), lambda i: (i, 0))
    full = lambda a: pl.BlockSpec(a.shape, lambda i: (0, 0))
    return pl.pallas_call(
        _kv_expand_kernel,
        grid=(n // tm,),
        in_specs=[row(MLA_KV_LORA), row(LANES), full(wuk), full(wuv)],
        out_specs=[row(2 * LANES * N_PAIRS), row(LANES * N_PAIRS)],
        out_shape=[jax.ShapeDtypeStruct((n, 2 * LANES * N_PAIRS), BF16),
                   jax.ShapeDtypeStruct((n, LANES * N_PAIRS), BF16)],
        compiler_params=_params("parallel"),
    )(ckv, kr128, wuk, wuv)


def _pair_query_masks(width):
    lane = lax.broadcasted_iota(jnp.int32, (1, width), 1)
    half = LANES // 2
    r0 = LANES
    m0 = (lane < half) | ((lane >= r0) & (lane < r0 + MLA_ROPE))
    m1 = ((lane >= half) & (lane < LANES)) | ((lane >= r0 + MLA_ROPE) & (lane < r0 + 2 * MLA_ROPE))
    return m0, m1


def _mla_prompt_attn_kernel(q_ref, k_ref, v_ref, o_ref, m_sc, l_sc, acc_sc):
    i = pl.program_id(1)
    tq, tk = ATTN_TQ, ATTN_TK
    q = q_ref[...]
    masks = _pair_query_masks(2 * LANES)
    qh = [jnp.where(m, q, jnp.zeros_like(q)) for m in masks]
    for h in range(2):
        m_sc[h] = jnp.full((tq, 1), NEG_INF, F32)
        l_sc[h] = jnp.zeros((tq, 1), F32)
        acc_sc[h] = jnp.zeros((tq, LANES), F32)

    def step(kb, visible):
        k = k_ref[pl.ds(pl.multiple_of(kb * tk, tk), tk), :]
        v = v_ref[pl.ds(pl.multiple_of(kb * tk, tk), tk), :]
        for h in range(2):
            s = _dot_nt(qh[h], k)
            if visible is not None:
                s = jnp.where(visible, s, NEG_INF)
            m_prev = m_sc[h]
            m_new = jnp.maximum(m_prev, jnp.max(s, -1, keepdims=True))
            alpha = jnp.exp(m_prev - m_new)
            p = jnp.exp(s - m_new)
            l_sc[h] = alpha * l_sc[h] + jnp.sum(p, -1, keepdims=True)
            acc_sc[h] = alpha * acc_sc[h] + _dot(p.astype(BF16), v)
            m_sc[h] = m_new

    ratio = tq // tk
    n_full = i * ratio

    def body(kb, c):
        step(kb, None)
        return c

    lax.fori_loop(0, n_full, body, 0)
    q_chunk = lax.broadcasted_iota(jnp.int32, (tq, tk), 0) // CHUNK
    k_chunk = lax.broadcasted_iota(jnp.int32, (tq, tk), 1) // CHUNK
    for d in range(ratio):
        step(n_full + d, (k_chunk + d * (tk // CHUNK)) <= q_chunk)

    lane = lax.broadcasted_iota(jnp.int32, (1, LANES), 1)
    o0 = acc_sc[0] / l_sc[0]
    o1 = acc_sc[1] / l_sc[1]
    o_ref[...] = jnp.where(lane < LANES // 2, o0, o1).astype(o_ref.dtype)


def _mla_prompt_attn(q, k, v):
    s = k.shape[0]
    tq = ATTN_TQ
    return pl.pallas_call(
        _mla_prompt_attn_kernel,
        grid=(N_PAIRS, s // tq),
        in_specs=[pl.BlockSpec((tq, 2 * LANES), lambda j, i: (i, j)),
                  pl.BlockSpec((s, 2 * LANES), lambda j, i: (0, j)),
                  pl.BlockSpec((s, LANES), lambda j, i: (0, j))],
        out_specs=pl.BlockSpec((tq, LANES), lambda j, i: (i, j)),
        out_shape=jax.ShapeDtypeStruct((s, LANES * N_PAIRS), BF16),
        scratch_shapes=[pltpu.VMEM((2, tq, 1), F32), pltpu.VMEM((2, tq, 1), F32),
                        pltpu.VMEM((2, tq, LANES), F32)],
        compiler_params=_params("parallel", "arbitrary"),
    )(q, k, v)


def _static_visible(q_pos, k_pos, left_chunks=None):
    qc = (q_pos // CHUNK)[:, None]
    kc = (k_pos // CHUNK)[None, :]
    vis = kc <= qc
    if left_chunks is not None:
        vis = vis & (kc >= qc - left_chunks) & (k_pos[None, :] >= 0)
    return vis


def _mla_sample_attn_kernel(q_ref, k_ref, v_ref, o_ref, *, visible):
    masks = _pair_query_masks(2 * LANES)
    lane = lax.broadcasted_iota(jnp.int32, (1, LANES), 1)
    vis = None if visible is None else jnp.asarray(visible)
    for j in range(N_PAIRS):
        q = q_ref[0, :, 2 * LANES * j:2 * LANES * (j + 1)]
        k = k_ref[0, :, 2 * LANES * j:2 * LANES * (j + 1)]
        v = v_ref[0, :, LANES * j:LANES * (j + 1)]
        outs = []
        for h in range(2):
            s = _dot_nt(jnp.where(masks[h], q, jnp.zeros_like(q)), k)
            if vis is not None:
                s = jnp.where(vis, s, NEG_INF)
            m = jnp.max(s, -1, keepdims=True)
            p = jnp.exp(s - m)
            l = jnp.sum(p, -1, keepdims=True)
            outs.append(_dot(p.astype(BF16), v) / l)
        o_ref[0, :, LANES * j:LANES * (j + 1)] = jnp.where(
            lane < LANES // 2, outs[0], outs[1]).astype(o_ref.dtype)


def _mla_sample_attn(q, k, v):
    b, t, _ = q.shape
    nk = k.shape[1]
    vis = _static_visible(PAST_LEN + np.arange(t), np.arange(nk))
    visible = None if vis.all() else vis
    blk = lambda a: pl.BlockSpec((1,) + a.shape[1:], lambda i: (i, 0, 0))
    return pl.pallas_call(
        functools.partial(_mla_sample_attn_kernel, visible=visible),
        grid=(b,),
        in_specs=[blk(q), blk(k), blk(v)],
        out_specs=pl.BlockSpec((1, t, LANES * N_PAIRS), lambda i: (i, 0, 0)),
        out_shape=jax.ShapeDtypeStruct((b, t, LANES * N_PAIRS), BF16),
        compiler_params=_params("parallel"),
    )(q, k, v)


def _out_ln_kernel(o_ref, w_ref, x_ref, g_ref, b_ref, y_ref):
    y = DEEPNORM_ALPHA * x_ref[...] + _dot(o_ref[...], w_ref[...])
    y_ref[...] = _layer_norm(y, g_ref[...], b_ref[...])


def _out_ln(o, w_o, x, g, b):
    n = x.shape[0]
    tm = TOKEN_TILE
    row = pl.BlockSpec((tm, D_MODEL), lambda i: (i, 0))
    full = lambda a: pl.BlockSpec(a.shape, lambda i: (0, 0))
    return pl.pallas_call(
        _out_ln_kernel,
        grid=(n // tm,),
        in_specs=[row, full(w_o), row, full(g), full(b)],
        out_specs=row,
        out_shape=jax.ShapeDtypeStruct((n, D_MODEL), F32),
        compiler_params=_params("parallel"),
    )(o, w_o, x, g, b)


def _band_qkv_kernel(x_ref, w_ref, q_ref, k_ref, v_ref, kf_ref, vf_ref, *, first_keep_tile):
    xb = x_ref[...].astype(BF16)
    qkv = _dot(xb, w_ref[...])
    q_ref[...] = (qkv[:, :D_MODEL] * (1.0 / math.sqrt(BAND_HEAD_DIM))).astype(BF16)
    k = qkv[:, D_MODEL:2 * D_MODEL]
    v = qkv[:, 2 * D_MODEL:]
    k_ref[...] = k.astype(BF16)
    v_ref[...] = v.astype(BF16)

    @pl.when(pl.program_id(0) >= first_keep_tile)
    def _():
        kf_ref[...] = k
        vf_ref[...] = v


def _band_qkv(x, w_qkv, n_keep):
    n = x.shape[0]
    tm = TOKEN_TILE
    first = (n - n_keep) // tm
    row = pl.BlockSpec((tm, D_MODEL), lambda i: (i, 0))
    keep = pl.BlockSpec((tm, D_MODEL), lambda i: (jnp.maximum(i - first, 0), 0))
    bf = jax.ShapeDtypeStruct((n, D_MODEL), BF16)
    kf = jax.ShapeDtypeStruct((n_keep, D_MODEL), F32)
    return pl.pallas_call(
        functools.partial(_band_qkv_kernel, first_keep_tile=first),
        grid=(n // tm,),
        in_specs=[row, pl.BlockSpec(w_qkv.shape, lambda i: (0, 0))],
        out_specs=[row, row, row, keep, keep],
        out_shape=[bf, bf, bf, kf, kf],
        compiler_params=_params("arbitrary"),
    )(x, w_qkv)


def _band_prompt_attn_kernel(q_ref, k_ref, v_ref, bias_ref, o_ref):
    i = pl.program_id(1)
    lane = lax.broadcasted_iota(jnp.int32, (1, LANES), 1)
    low = lane < LANES // 2
    key_idx = lax.broadcasted_iota(jnp.int32, (1, BAND_WIN), 1)

    def body(sb, c):
        r0 = pl.multiple_of(i * BAND_QBLOCK + sb * BAND_QSUB, BAND_QSUB)
        q = q_ref[pl.ds(pl.multiple_of(sb * BAND_QSUB, BAND_QSUB), BAND_QSUB), :]
        kw = k_ref[pl.ds(r0, BAND_WIN), :]
        vw = v_ref[pl.ds(r0, BAND_WIN), :]
        in_seq = key_idx >= BAND_KEEP_MAX - r0
        outs = []
        for h in range(2):
            qh = jnp.where(low if h == 0 else ~low, q, jnp.zeros_like(q))
            s = _dot_nt(qh, kw) + bias_ref[h]
            s = jnp.where(in_seq, s, NEG_INF)
            m = jnp.max(s, -1, keepdims=True)
            p = jnp.exp(s - m)
            l = jnp.sum(p, -1, keepdims=True)
            outs.append(_dot(p.astype(BF16), vw) / l)
        o_ref[pl.ds(pl.multiple_of(sb * BAND_QSUB, BAND_QSUB), BAND_QSUB), :] = jnp.where(
            low, outs[0], outs[1]).astype(o_ref.dtype)
        return c

    lax.fori_loop(0, BAND_QBLOCK // BAND_QSUB, body, 0)


def _band_prompt_attn(q, k_pad, v_pad, bias):
    sp = k_pad.shape[0]
    s = sp - BAND_KEEP_MAX
    return pl.pallas_call(
        _band_prompt_attn_kernel,
        grid=(N_PAIRS, s // BAND_QBLOCK),
        in_specs=[pl.BlockSpec((BAND_QBLOCK, LANES), lambda j, i: (i, j)),
                  pl.BlockSpec((sp, LANES), lambda j, i: (0, j)),
                  pl.BlockSpec((sp, LANES), lambda j, i: (0, j)),
                  pl.BlockSpec((2, BAND_QSUB, BAND_WIN), lambda j, i: (j, 0, 0))],
        out_specs=pl.BlockSpec((BAND_QBLOCK, LANES), lambda j, i: (i, j)),
        out_shape=jax.ShapeDtypeStruct((s, D_MODEL), BF16),
        compiler_params=_params("parallel", "arbitrary"),
    )(q, k_pad, v_pad, bias)


def _band_sample_attn_kernel(q_ref, kc_ref, vc_ref, kn_ref, vn_ref, bc_ref, bn_ref, o_ref):
    lane = lax.broadcasted_iota(jnp.int32, (1, LANES), 1)
    low = lane < LANES // 2
    for j in range(N_PAIRS):
        cols = slice(LANES * j, LANES * (j + 1))
        q = q_ref[0, :, cols]
        kc = kc_ref[0, :, cols].astype(BF16)
        vc = vc_ref[0, :, cols].astype(BF16)
        kn = kn_ref[0, :, cols]
        vn = vn_ref[0, :, cols]
        outs = []
        for h in range(2):
            qh = jnp.where(low if h == 0 else ~low, q, jnp.zeros_like(q))
            sc = _dot_nt(qh, kc) + bc_ref[2 * j + h]
            sn = _dot_nt(qh, kn) + bn_ref[2 * j + h]
            m = jnp.maximum(jnp.max(sc, -1, keepdims=True), jnp.max(sn, -1, keepdims=True))
            pc = jnp.exp(sc - m)
            pn = jnp.exp(sn - m)
            l = jnp.sum(pc, -1, keepdims=True) + jnp.sum(pn, -1, keepdims=True)
            outs.append((_dot(pc.astype(BF16), vc) + _dot(pn.astype(BF16), vn)) / l)
        o_ref[0, :, cols] = jnp.where(low, outs[0], outs[1]).astype(o_ref.dtype)


def _band_sample_attn(q, kc, vc, kn, vn, bias_c, bias_n):
    b, t, _ = q.shape
    blk = lambda a: pl.BlockSpec((1,) + a.shape[1:], lambda i: (i, 0, 0))
    full = lambda a: pl.BlockSpec(a.shape, lambda i: (0, 0, 0))
    return pl.pallas_call(
        _band_sample_attn_kernel,
        grid=(b,),
        in_specs=[blk(q), blk(kc), blk(vc), blk(kn), blk(vn), full(bias_c), full(bias_n)],
        out_specs=pl.BlockSpec((1, t, D_MODEL), lambda i: (i, 0, 0)),
        out_shape=jax.ShapeDtypeStruct((b, t, D_MODEL), BF16),
        compiler_params=_params("parallel"),
    )(q, kc, vc, kn, vn, bias_c, bias_n)


def _band_bias(rel_table, q_pos, k_pos):
    rel = np.clip(q_pos[:, None] - k_pos[None, :], -REL_MAX, REL_MAX) + REL_MAX
    vis = _static_visible(q_pos, k_pos, LEFT_CHUNKS)
    return jnp.where(jnp.asarray(vis)[None], rel_table.astype(F32)[:, rel], NEG_INF)


def _route(logits_t, rb_ref):
    scores = jax.nn.sigmoid(logits_t)
    rows = [scores[e:e + 1, :] for e in range(N_EXPERTS)]
    biased = [rows[e] + rb_ref[e] for e in range(N_EXPERTS)]
    best = None
    sel = None
    for g in range(N_GROUPS):
        a, b, c, d = biased[EXPERTS_PER_GROUP * g:EXPERTS_PER_GROUP * (g + 1)]
        top2 = jnp.maximum(jnp.maximum(jnp.maximum(a + b, a + c), jnp.maximum(a + d, b + c)),
                           jnp.maximum(b + d, c + d))
        if g == 0:
            best, sel = top2, jnp.zeros(top2.shape, jnp.int32)
        else:
            better = top2 > best
            sel = jnp.where(better, g, sel)
            best = jnp.where(better, top2, best)
    cand = [jnp.where(sel == e // EXPERTS_PER_GROUP, biased[e], NEG_INF) for e in range(N_EXPERTS)]

    def argmax_first(vals):
        top = functools.reduce(jnp.maximum, vals)
        idx = jnp.full(top.shape, N_EXPERTS, jnp.int32)
        for e in reversed(range(N_EXPERTS)):
            idx = jnp.where(vals[e] == top, e, idx)
        return idx

    i1 = argmax_first(cand)
    i2 = argmax_first([jnp.where(i1 == e, -jnp.inf, cand[e]) for e in range(N_EXPERTS)])
    picked = [jnp.where((i1 == e) | (i2 == e), rows[e], 0.0) for e in range(N_EXPERTS)]
    total = functools.reduce(jnp.add, picked)
    return [p / total for p in picked]


def _moe_kernel(x_ref, rw_ref, rb_ref, wg_ref, wu_ref, wd_ref, g_ref, b_ref, y_ref,
                xb_sc, comb_sc, acc_sc):
    grp = pl.program_id(1)

    @pl.when(grp == 0)
    def _():
        x = x_ref[...]
        xb = x.astype(BF16)
        xb_sc[...] = xb
        acc_sc[...] = DEEPNORM_ALPHA * x
        comb_rows = _route(_dot_nt(rw_ref[...], xb), rb_ref)
        comb = jnp.concatenate(comb_rows, axis=0).T
        for g in range(N_GROUPS):
            comb_sc[g] = comb[:, EXPERTS_PER_GROUP * g:EXPERTS_PER_GROUP * (g + 1)]

    xb = xb_sc[...]
    gate = _dot(xb, wg_ref[0])
    up = _dot(xb, wu_ref[0])
    comb = comb_sc[grp]
    parts = []
    for e in range(EXPERTS_PER_GROUP):
        cols = slice(D_EXPERT * e, D_EXPERT * (e + 1))
        h = jax.nn.silu(gate[:, cols]) * up[:, cols]
        parts.append((h * comb[:, e:e + 1]).astype(BF16))
    acc_sc[...] += _dot(jnp.concatenate(parts, axis=1), wd_ref[0])

    @pl.when(grp == N_GROUPS - 1)
    def _():
        y_ref[...] = _layer_norm(acc_sc[...], g_ref[...], b_ref[...])


def _moe_ln(x, rw_t, rb, wg, wu, wd, g, b):
    n = x.shape[0]
    tm = MOE_TILE
    row = pl.BlockSpec((tm, D_MODEL), lambda i, j: (i, 0))
    full = lambda a: pl.BlockSpec(a.shape, lambda i, j: (0, 0))
    grp = lambda a: pl.BlockSpec((1,) + a.shape[1:], lambda i, j: (j, 0, 0))
    return pl.pallas_call(
        _moe_kernel,
        grid=(n // tm, N_GROUPS),
        in_specs=[row, full(rw_t), pl.BlockSpec(memory_space=pltpu.SMEM),
                  grp(wg), grp(wu), grp(wd), full(g), full(b)],
        out_specs=row,
        out_shape=jax.ShapeDtypeStruct((n, D_MODEL), F32),
        scratch_shapes=[pltpu.VMEM((tm, D_MODEL), BF16),
                        pltpu.VMEM((N_GROUPS, tm, EXPERTS_PER_GROUP), F32),
                        pltpu.VMEM((tm, D_MODEL), F32)],
        compiler_params=_params("parallel", "arbitrary"),
    )(x, rw_t, rb, wg, wu, wd, g, b)


def _rope_tables():
    half = MLA_ROPE // 2
    inv = ROPE_THETA ** (-jnp.arange(half, dtype=F32) / half)
    pos = jnp.concatenate([jnp.arange(SEQ, dtype=jnp.int32),
                           jnp.tile(PAST_LEN + jnp.arange(DEC_SEQ, dtype=jnp.int32), DEC_BATCH)])
    ang = pos.astype(F32)[:, None] * inv[None, :]
    reps = LANES // half
    return jnp.tile(jnp.cos(ang), (1, reps)), jnp.tile(jnp.sin(ang), (1, reps))


def _rot_cols(w):
    half = MLA_ROPE // 2
    return jnp.concatenate([-w[..., half:], w[..., :half]], -1)


def _mla_weights(w_in, w_uq, w_uk, w_uv, w_o):
    o_kv = MLA_Q_LORA + MLA_KV_LORA
    w_r = w_in[:, o_kv:]
    zpad = jnp.zeros((D_MODEL, LANES - 2 * MLA_ROPE), F32)
    w_in_p = jnp.concatenate([w_in[:, :o_kv], w_r, w_r, zpad,
                              _rot_cols(w_r), _rot_cols(w_r), zpad], 1).astype(BF16)
    wq = w_uq.reshape(MLA_Q_LORA, N_PAIRS, 2, MLA_QK)
    nope = wq[..., :MLA_NOPE].reshape(MLA_Q_LORA, N_PAIRS, 2 * MLA_NOPE)
    ropew = wq[..., MLA_NOPE:]
    zq = jnp.zeros((MLA_Q_LORA, N_PAIRS, LANES - 2 * MLA_ROPE), F32)
    rope_cols = jnp.concatenate([ropew.reshape(MLA_Q_LORA, N_PAIRS, 2 * MLA_ROPE), zq], -1)
    rot_cols = jnp.concatenate([_rot_cols(ropew).reshape(MLA_Q_LORA, N_PAIRS, 2 * MLA_ROPE), zq], -1)
    wqa = jnp.concatenate([nope, rope_cols], -1).reshape(MLA_Q_LORA, 2 * LANES * N_PAIRS).astype(BF16)
    wqb = rot_cols.reshape(MLA_Q_LORA, LANES * N_PAIRS).astype(BF16)
    return w_in_p, wqa, wqb, w_uk.astype(BF16), w_uv.astype(BF16), w_o.astype(BF16)


def _group_experts(w_gate, w_up, w_down):
    d, f = D_MODEL, D_EXPERT
    up_layout = lambda w: (w.reshape(N_GROUPS, EXPERTS_PER_GROUP, d, f).transpose(0, 2, 1, 3)
                           .reshape(N_GROUPS, d, EXPERTS_PER_GROUP * f).astype(BF16))
    wd = w_down.reshape(N_GROUPS, EXPERTS_PER_GROUP * f, d).astype(BF16)
    return up_layout(w_gate), up_layout(w_up), wd


def kernel(x_prompt, x_sample, cache_mla_ckv, cache_mla_krope, cache_band_k, cache_band_v,
           mla_w_in, mla_q_norm, mla_kv_norm, mla_w_uq, mla_w_uk, mla_w_uv, mla_w_o,
           band_w_qkv, band_rel_bias, band_w_o,
           router_w, router_b, moe_w_gate, moe_w_up, moe_w_down,
           ln_mix_g, ln_mix_b, ln_ffn_g, ln_ffn_b):
    x = jnp.concatenate([x_prompt.reshape(SEQ, D_MODEL), x_sample.reshape(N_SAMPLE, D_MODEL)], 0)
    cos, sin = _rope_tables()
    rw_t = router_w.T.astype(BF16)
    rb = router_b.astype(F32)
    band_keep = cache_band_k.shape[2]
    row2 = lambda a: a.reshape(1, -1)

    ckv_p, kr_p, ckv_s, kr_s = [], [], [], []
    bk_p, bv_p, bk_s, bv_s = [], [], [], []
    for i in range(DEPTH):
        j = i // N_MIXERS
        if i % N_MIXERS == 0:
            w_in_p, wqa, wqb, wuk, wuv, w_o = _mla_weights(
                mla_w_in[j], mla_w_uq[j], mla_w_uk[j], mla_w_uv[j], mla_w_o[j])
            ckv, kr128, q = _mla_proj(x, w_in_p, row2(mla_q_norm[j]), row2(mla_kv_norm[j]),
                                      wqa, wqb, cos, sin)
            kr = kr128[:, :MLA_ROPE]
            ckv_p.append(ckv[:SEQ].reshape(1, SEQ, MLA_KV_LORA))
            kr_p.append(kr[:SEQ].reshape(1, SEQ, MLA_ROPE))
            ckv_s.append(ckv[SEQ:].reshape(DEC_BATCH, DEC_SEQ, MLA_KV_LORA))
            kr_s.append(kr[SEQ:].reshape(DEC_BATCH, DEC_SEQ, MLA_ROPE))
            k_p, v_p = _kv_expand(ckv, kr128, wuk, wuv, SEQ)
            o_p = _mla_prompt_attn(q, k_p, v_p)
            cache_kr = cache_mla_krope[j]
            cache_kr128 = jnp.concatenate(
                [cache_kr, cache_kr, jnp.zeros(cache_kr.shape[:2] + (LANES - 2 * MLA_ROPE,), F32)], -1)
            n_ctx = PAST_LEN + DEC_SEQ
            ckv_ctx = jnp.concatenate([cache_mla_ckv[j], ckv_s[-1]], 1).reshape(DEC_BATCH * n_ctx, -1)
            kr_ctx = jnp.concatenate([cache_kr128, kr128[SEQ:].reshape(DEC_BATCH, DEC_SEQ, LANES)], 1)
            k_s, v_s = _kv_expand(ckv_ctx, kr_ctx.reshape(DEC_BATCH * n_ctx, LANES), wuk, wuv,
                                  DEC_BATCH * n_ctx)
            o_s = _mla_sample_attn(q[SEQ:].reshape(DEC_BATCH, DEC_SEQ, -1),
                                   k_s.reshape(DEC_BATCH, n_ctx, -1), v_s.reshape(DEC_BATCH, n_ctx, -1))
            o = jnp.concatenate([o_p, o_s.reshape(N_SAMPLE, D_MODEL)], 0)
        else:
            n_keep = min(BAND_KEEP_MAX, SEQ) + N_SAMPLE
            q, k, v, k_f, v_f = _band_qkv(x, band_w_qkv[j].astype(BF16), n_keep)
            keep = n_keep - N_SAMPLE
            hd = (BAND_HEADS, BAND_HEAD_DIM)
            bk_p.append(k_f[:keep].reshape((1, keep) + hd))
            bv_p.append(v_f[:keep].reshape((1, keep) + hd))
            bk_s.append(k_f[keep:].reshape((DEC_BATCH, DEC_SEQ) + hd))
            bv_s.append(v_f[keep:].reshape((DEC_BATCH, DEC_SEQ) + hd))
            w_o = band_w_o[j].astype(BF16)
            pad = ((BAND_KEEP_MAX, 0), (0, 0))
            q_off = np.arange(BAND_QSUB)
            bias_p = _band_bias(band_rel_bias[j], BAND_KEEP_MAX + q_off, np.arange(BAND_WIN))
            o_p = _band_prompt_attn(q, jnp.pad(k[:SEQ], pad), jnp.pad(v[:SEQ], pad), bias_p)
            q_pos = PAST_LEN + np.arange(DEC_SEQ)
            pos_c = PAST_LEN - band_keep + np.arange(band_keep)
            bias_c = _band_bias(band_rel_bias[j], q_pos, pos_c)
            bias_n = _band_bias(band_rel_bias[j], q_pos, q_pos)
            s3 = lambda a: a[SEQ:].reshape(DEC_BATCH, DEC_SEQ, D_MODEL)
            o_s = _band_sample_attn(s3(q), cache_band_k[j].reshape(DEC_BATCH, band_keep, D_MODEL),
                                    cache_band_v[j].reshape(DEC_BATCH, band_keep, D_MODEL),
                                    s3(k), s3(v), bias_c, bias_n)
            o = jnp.concatenate([o_p, o_s.reshape(N_SAMPLE, D_MODEL)], 0)
        x = _out_ln(o, w_o, x, row2(ln_mix_g[i]), row2(ln_mix_b[i]))
        wg, wu, wd = _group_experts(moe_w_gate[i], moe_w_up[i], moe_w_down[i])
        x = _moe_ln(x, rw_t, rb, wg, wu, wd, row2(ln_ffn_g[i]), row2(ln_ffn_b[i]))

    return (x[:SEQ].reshape(1, SEQ, D_MODEL), x[SEQ:].reshape(DEC_BATCH, DEC_SEQ, D_MODEL),
            jnp.stack(ckv_p, 0), jnp.stack(kr_p, 0), jnp.stack(ckv_s, 0), jnp.stack(kr_s, 0),
            jnp.stack(bk_p, 0), jnp.stack(bv_p, 0), jnp.stack(bk_s, 0), jnp.stack(bv_s, 0))
```

```python
import functools
import math

import numpy as np
import jax
import jax.numpy as jnp
from jax import lax
from jax.experimental import pallas as pl
from jax.experimental.pallas import tpu as pltpu

D_MODEL = 1024
SEQ = 16384
DEPTH = 2
DEC_BATCH = 32
DEC_SEQ = 32
PAST_LEN = 1024
CHUNK = 64
N_MIXERS = 2

MLA_HEADS = 16
MLA_Q_LORA = 384
MLA_KV_LORA = 256
MLA_NOPE = 64
MLA_ROPE = 32
MLA_QK = MLA_NOPE + MLA_ROPE
MLA_V = 64
ROPE_THETA = 10000.0

BAND_HEADS = 16
BAND_HEAD_DIM = D_MODEL // BAND_HEADS
LEFT_CHUNKS = 8
BAND_KEEP_MAX = LEFT_CHUNKS * CHUNK
REL_MAX = 128

N_EXPERTS = 16
N_GROUPS = 4
EXPERTS_PER_GROUP = N_EXPERTS // N_GROUPS
D_EXPERT = 256

DEEPNORM_ALPHA = (2.0 * DEPTH) ** 0.25
NORM_EPS = 1e-5
NEG_INF = -1e30

N_SAMPLE = DEC_BATCH * DEC_SEQ
N_TOKENS = SEQ + N_SAMPLE
N_PAIRS = MLA_HEADS // 2
LANES = 128
VMEM_LIMIT = 56 * 1024 * 1024

TOKEN_TILE = 512
MOE_TILE = 512
ATTN_TQ = 512
ATTN_TK = 512
BAND_QSUB = 128
BAND_WIN = BAND_KEEP_MAX + BAND_QSUB
BAND_QBLOCK = 1024

BF16 = jnp.bfloat16
F32 = jnp.float32


def _params(*sem):
    return pltpu.CompilerParams(dimension_semantics=sem, vmem_limit_bytes=VMEM_LIMIT)


def _dot(a, b):
    return jnp.dot(a, b, preferred_element_type=F32)


def _dot_nt(a, b):
    return lax.dot_general(a, b, (((1,), (1,)), ((), ())), preferred_element_type=F32)


def _layer_norm(y, g, b):
    mu = jnp.mean(y, -1, keepdims=True)
    d = y - mu
    var = jnp.mean(d * d, -1, keepdims=True)
    return d * lax.rsqrt(var + NORM_EPS) * g + b


def _rms_norm(y, g):
    return y * lax.rsqrt(jnp.mean(y * y, -1, keepdims=True) + NORM_EPS) * g


def _mla_proj_kernel(x_ref, w_in_ref, qn_ref, kvn_ref, wqa_ref, wqb_ref, cos_ref, sin_ref,
                     ckv_ref, kr_ref, q_ref):
    xb = x_ref[...].astype(BF16)
    a = _dot(xb, w_in_ref[...])
    cq = _rms_norm(a[:, :MLA_Q_LORA], qn_ref[...])
    o_kv = MLA_Q_LORA + MLA_KV_LORA
    ckv_ref[...] = _rms_norm(a[:, MLA_Q_LORA:o_kv], kvn_ref[...])
    cos = cos_ref[...]
    sin = sin_ref[...]
    kr_ref[...] = a[:, o_kv:o_kv + LANES] * cos + a[:, o_kv + LANES:] * sin
    cqb = cq.astype(BF16)
    qa = _dot(cqb, wqa_ref[...])
    qb = _dot(cqb, wqb_ref[...])
    scale = math.log2(math.e) / math.sqrt(MLA_QK)
    for j in range(N_PAIRS):
        c0 = 2 * LANES * j
        q_ref[:, c0:c0 + LANES] = (qa[:, c0:c0 + LANES] * scale).astype(BF16)
        rot = qa[:, c0 + LANES:c0 + 2 * LANES] * cos + qb[:, LANES * j:LANES * (j + 1)] * sin
        q_ref[:, c0 + LANES:c0 + 2 * LANES] = (rot * scale).astype(BF16)


def _mla_proj(x, w_in_p, qn, kvn, wqa, wqb, cos, sin):
    n = x.shape[0]
    tm = TOKEN_TILE
    row = lambda w: pl.BlockSpec((tm, w), lambda i: (i, 0))
    full = lambda a: pl.BlockSpec(a.shape, lambda i: (0, 0))
    return pl.pallas_call(
        _mla_proj_kernel,
        grid=(n // tm,),
        in_specs=[row(D_MODEL), full(w_in_p), full(qn), full(kvn), full(wqa), full(wqb),
                  row(LANES), row(LANES)],
        out_specs=[row(MLA_KV_LORA), row(LANES), row(2 * LANES * N_PAIRS)],
        out_shape=[jax.ShapeDtypeStruct((n, MLA_KV_LORA), F32),
                   jax.ShapeDtypeStruct((n, LANES), F32),
                   jax.ShapeDtypeStruct((n, 2 * LANES * N_PAIRS), BF16)],
        compiler_params=_params("parallel"),
        name="mla_proj",
    )(x, w_in_p, qn, kvn, wqa, wqb, cos, sin)


def _kv_expand_kernel(ckv_ref, kr_ref, wuk_ref, wuv_ref, k_ref, v_ref, *, transpose_v):
    cb = ckv_ref[...].astype(BF16)
    kn = _dot(cb, wuk_ref[...])
    if transpose_v:
        v_ref[0] = _dot_nt(wuv_ref[...], cb).astype(BF16)
    else:
        v_ref[...] = _dot(cb, wuv_ref[...]).astype(BF16)
    krb = kr_ref[...].astype(BF16)
    for j in range(N_PAIRS):
        c0 = 2 * LANES * j
        k_ref[:, c0:c0 + LANES] = kn[:, LANES * j:LANES * (j + 1)].astype(BF16)
        k_ref[:, c0 + LANES:c0 + 2 * LANES] = krb


def _kv_expand(ckv, kr128, wuk, wuv, n, transpose_v=False):
    tm = ATTN_TK if transpose_v else TOKEN_TILE
    row = lambda w: pl.BlockSpec((tm, w), lambda i: (i, 0))
    full = lambda a: pl.BlockSpec(a.shape, lambda i: (0, 0))
    nv = LANES * N_PAIRS
    if transpose_v:
        v_spec = pl.BlockSpec((1, nv, tm), lambda i: (i, 0, 0))
        v_shape = jax.ShapeDtypeStruct((n // tm, nv, tm), BF16)
    else:
        v_spec = row(nv)
        v_shape = jax.ShapeDtypeStruct((n, nv), BF16)
    return pl.pallas_call(
        functools.partial(_kv_expand_kernel, transpose_v=transpose_v),
        grid=(n // tm,),
        in_specs=[row(MLA_KV_LORA), row(LANES), full(wuk), full(wuv)],
        out_specs=[row(2 * LANES * N_PAIRS), v_spec],
        out_shape=[jax.ShapeDtypeStruct((n, 2 * LANES * N_PAIRS), BF16), v_shape],
        compiler_params=_params("parallel"),
        name="mla_kv_expand_t" if transpose_v else "mla_kv_expand",
    )(ckv, kr128, wuk, wuv)


def _pair_query_masks(width):
    lane = lax.broadcasted_iota(jnp.int32, (1, width), 1)
    half = LANES // 2
    r0 = LANES
    m0 = (lane < half) | ((lane >= r0) & (lane < r0 + MLA_ROPE))
    m1 = ((lane >= half) & (lane < LANES)) | ((lane >= r0 + MLA_ROPE) & (lane < r0 + 2 * MLA_ROPE))
    return m0, m1


def _mla_prompt_attn_kernel(q_ref, k_ref, vt_ref, o_ref, qh_sc, s_sc, m_sc, l_sc, acc_sc):
    i = pl.program_id(1)
    tq, tk = ATTN_TQ, ATTN_TK
    half = LANES // 2
    q = q_ref[...]
    for h, mask in enumerate(_pair_query_masks(2 * LANES)):
        qh_sc[h] = jnp.where(mask, q, jnp.zeros_like(q))
        m_sc[h] = jnp.full((1, tq), NEG_INF, F32)
        l_sc[h] = jnp.zeros((1, tq), F32)
        acc_sc[h] = jnp.zeros((half, tq), F32)

    def scores(kb, h, slot):
        k = k_ref[pl.ds(pl.multiple_of(kb * tk, tk), tk), :]
        s_sc[h, slot] = _dot_nt(k, qh_sc[h])

    def update(kb, h, slot, visible=None):
        s = s_sc[h, slot]
        if visible is not None:
            s = jnp.where(visible, s, NEG_INF)
        m_prev = m_sc[h]
        m_new = jnp.maximum(m_prev, jnp.max(s, 0, keepdims=True))
        alpha = jnp.exp2(m_prev - m_new)
        p = jnp.exp2(s - m_new)
        l_sc[h] = alpha * l_sc[h] + jnp.sum(p, 0, keepdims=True)
        vt = vt_ref[kb, half * h:half * (h + 1), :]
        acc_sc[h] = alpha * acc_sc[h] + _dot(vt, p.astype(BF16))
        m_sc[h] = m_new

    scores(0, 0, 0)
    scores(0, 1, 0)

    def advance(kb, slot):
        for h in range(2):
            scores(kb + 1, h, 1 - slot)
            update(kb, h, slot)

    def body(kk, c):
        advance(2 * kk, 0)
        advance(2 * kk + 1, 1)
        return c

    lax.fori_loop(0, i // 2, body, 0)

    def diagonal(slot):
        k_chunk = lax.broadcasted_iota(jnp.int32, (tk, tq), 0) // CHUNK
        q_chunk = lax.broadcasted_iota(jnp.int32, (tk, tq), 1) // CHUNK
        for h in range(2):
            update(i, h, slot, k_chunk <= q_chunk)

    @pl.when(i % 2 == 0)
    def _():
        diagonal(0)

    @pl.when(i % 2 == 1)
    def _():
        advance(i - 1, 0)
        diagonal(1)

    o_t = jnp.concatenate([acc_sc[0] / l_sc[0], acc_sc[1] / l_sc[1]], axis=0)
    o_ref[...] = o_t.T.astype(o_ref.dtype)


def _mla_prompt_attn(q, k, vt):
    s = k.shape[0]
    tq = ATTN_TQ
    assert ATTN_TQ == ATTN_TK and vt.shape == (s // ATTN_TK, LANES * N_PAIRS, ATTN_TK)
    return pl.pallas_call(
        _mla_prompt_attn_kernel,
        grid=(N_PAIRS, s // tq),
        in_specs=[pl.BlockSpec((tq, 2 * LANES), lambda j, i: (i, j)),
                  pl.BlockSpec((s, 2 * LANES), lambda j, i: (0, j)),
                  pl.BlockSpec((s // ATTN_TK, LANES, ATTN_TK), lambda j, i: (0, j, 0))],
        out_specs=pl.BlockSpec((tq, LANES), lambda j, i: (i, j)),
        out_shape=jax.ShapeDtypeStruct((s, LANES * N_PAIRS), BF16),
        scratch_shapes=[pltpu.VMEM((2, tq, 2 * LANES), BF16),
                        pltpu.VMEM((2, 2, ATTN_TK, tq), F32),
                        pltpu.VMEM((2, 1, tq), F32), pltpu.VMEM((2, 1, tq), F32),
                        pltpu.VMEM((2, LANES // 2, tq), F32)],
        compiler_params=_params("parallel", "arbitrary"),
        name="mla_prompt_attn",
    )(q, k, vt)


def _static_visible(q_pos, k_pos, left_chunks=None):
    qc = (q_pos // CHUNK)[:, None]
    kc = (k_pos // CHUNK)[None, :]
    vis = kc <= qc
    if left_chunks is not None:
        vis = vis & (kc >= qc - left_chunks) & (k_pos[None, :] >= 0)
    return vis


def _mla_sample_attn_kernel(q_ref, k_ref, v_ref, o_ref, *, visible):
    masks = _pair_query_masks(2 * LANES)
    lane = lax.broadcasted_iota(jnp.int32, (1, LANES), 1)
    vis = None if visible is None else jnp.asarray(visible)
    for j in range(N_PAIRS):
        q = q_ref[0, :, 2 * LANES * j:2 * LANES * (j + 1)]
        k = k_ref[0, :, 2 * LANES * j:2 * LANES * (j + 1)]
        v = v_ref[0, :, LANES * j:LANES * (j + 1)]
        outs = []
        for h in range(2):
            s = _dot_nt(jnp.where(masks[h], q, jnp.zeros_like(q)), k)
            if vis is not None:
                s = jnp.where(vis, s, NEG_INF)
            m = jnp.max(s, -1, keepdims=True)
            p = jnp.exp2(s - m)
            l = jnp.sum(p, -1, keepdims=True)
            outs.append(_dot(p.astype(BF16), v) / l)
        o_ref[0, :, LANES * j:LANES * (j + 1)] = jnp.where(
            lane < LANES // 2, outs[0], outs[1]).astype(o_ref.dtype)


def _mla_sample_attn(q, k, v):
    b, t, _ = q.shape
    nk = k.shape[1]
    vis = _static_visible(PAST_LEN + np.arange(t), np.arange(nk))
    visible = None if vis.all() else vis
    blk = lambda a: pl.BlockSpec((1,) + a.shape[1:], lambda i: (i, 0, 0))
    return pl.pallas_call(
        functools.partial(_mla_sample_attn_kernel, visible=visible),
        grid=(b,),
        in_specs=[blk(q), blk(k), blk(v)],
        out_specs=pl.BlockSpec((1, t, LANES * N_PAIRS), lambda i: (i, 0, 0)),
        out_shape=jax.ShapeDtypeStruct((b, t, LANES * N_PAIRS), BF16),
        compiler_params=_params("parallel"),
        name="mla_sample_attn",
    )(q, k, v)


def _out_ln_kernel(o_ref, w_ref, x_ref, g_ref, b_ref, y_ref):
    y = DEEPNORM_ALPHA * x_ref[...] + _dot(o_ref[...], w_ref[...])
    y_ref[...] = _layer_norm(y, g_ref[...], b_ref[...])


def _out_ln(o, w_o, x, g, b):
    n = x.shape[0]
    tm = TOKEN_TILE
    row = pl.BlockSpec((tm, D_MODEL), lambda i: (i, 0))
    full = lambda a: pl.BlockSpec(a.shape, lambda i: (0, 0))
    return pl.pallas_call(
        _out_ln_kernel,
        grid=(n // tm,),
        in_specs=[row, full(w_o), row, full(g), full(b)],
        out_specs=row,
        out_shape=jax.ShapeDtypeStruct((n, D_MODEL), F32),
        compiler_params=_params("parallel"),
        name="out_proj_ln",
    )(o, w_o, x, g, b)


def _band_qkv_kernel(x_ref, w_ref, q_ref, k_ref, v_ref, kf_ref, vf_ref, *, first_keep_tile):
    xb = x_ref[...].astype(BF16)
    qkv = _dot(xb, w_ref[...])
    q_ref[...] = (qkv[:, :D_MODEL] * (1.0 / math.sqrt(BAND_HEAD_DIM))).astype(BF16)
    k = qkv[:, D_MODEL:2 * D_MODEL]
    v = qkv[:, 2 * D_MODEL:]
    k_ref[...] = k.astype(BF16)
    v_ref[...] = v.astype(BF16)

    @pl.when(pl.program_id(0) >= first_keep_tile)
    def _():
        kf_ref[...] = k
        vf_ref[...] = v


def _band_qkv(x, w_qkv, n_keep):
    n = x.shape[0]
    tm = TOKEN_TILE
    first = (n - n_keep) // tm
    row = pl.BlockSpec((tm, D_MODEL), lambda i: (i, 0))
    keep = pl.BlockSpec((tm, D_MODEL), lambda i: (jnp.maximum(i - first, 0), 0))
    bf = jax.ShapeDtypeStruct((n, D_MODEL), BF16)
    kf = jax.ShapeDtypeStruct((n_keep, D_MODEL), F32)
    return pl.pallas_call(
        functools.partial(_band_qkv_kernel, first_keep_tile=first),
        grid=(n // tm,),
        in_specs=[row, pl.BlockSpec(w_qkv.shape, lambda i: (0, 0))],
        out_specs=[row, row, row, keep, keep],
        out_shape=[bf, bf, bf, kf, kf],
        compiler_params=_params("arbitrary"),
        name="band_qkv",
    )(x, w_qkv)


def _band_prompt_attn_kernel(q_ref, k_ref, v_ref, bias_ref, o_ref):
    i = pl.program_id(1)
    lane = lax.broadcasted_iota(jnp.int32, (1, LANES), 1)
    low = lane < LANES // 2
    key_idx = lax.broadcasted_iota(jnp.int32, (1, BAND_WIN), 1)

    def body(sb, c):
        r0 = pl.multiple_of(i * BAND_QBLOCK + sb * BAND_QSUB, BAND_QSUB)
        q = q_ref[pl.ds(pl.multiple_of(sb * BAND_QSUB, BAND_QSUB), BAND_QSUB), :]
        kw = k_ref[pl.ds(r0, BAND_WIN), :]
        vw = v_ref[pl.ds(r0, BAND_WIN), :]
        in_seq = key_idx >= BAND_KEEP_MAX - r0
        outs = []
        for h in range(2):
            qh = jnp.where(low if h == 0 else ~low, q, jnp.zeros_like(q))
            s = _dot_nt(qh, kw) + bias_ref[h]
            s = jnp.where(in_seq, s, NEG_INF)
            m = jnp.max(s, -1, keepdims=True)
            p = jnp.exp(s - m)
            l = jnp.sum(p, -1, keepdims=True)
            outs.append(_dot(p.astype(BF16), vw) / l)
        o_ref[pl.ds(pl.multiple_of(sb * BAND_QSUB, BAND_QSUB), BAND_QSUB), :] = jnp.where(
            low, outs[0], outs[1]).astype(o_ref.dtype)
        return c

    lax.fori_loop(0, BAND_QBLOCK // BAND_QSUB, body, 0)


def _band_prompt_attn(q, k_pad, v_pad, bias):
    sp = k_pad.shape[0]
    s = sp - BAND_KEEP_MAX
    return pl.pallas_call(
        _band_prompt_attn_kernel,
        grid=(N_PAIRS, s // BAND_QBLOCK),
        in_specs=[pl.BlockSpec((BAND_QBLOCK, LANES), lambda j, i: (i, j)),
                  pl.BlockSpec((sp, LANES), lambda j, i: (0, j)),
                  pl.BlockSpec((sp, LANES), lambda j, i: (0, j)),
                  pl.BlockSpec((2, BAND_QSUB, BAND_WIN), lambda j, i: (j, 0, 0))],
        out_specs=pl.BlockSpec((BAND_QBLOCK, LANES), lambda j, i: (i, j)),
        out_shape=jax.ShapeDtypeStruct((s, D_MODEL), BF16),
        compiler_params=_params("parallel", "arbitrary"),
        name="band_prompt_attn",
    )(q, k_pad, v_pad, bias)


def _band_sample_attn_kernel(q_ref, kc_ref, vc_ref, kn_ref, vn_ref, bc_ref, bn_ref, o_ref):
    lane = lax.broadcasted_iota(jnp.int32, (1, LANES), 1)
    low = lane < LANES // 2
    for j in range(N_PAIRS):
        cols = slice(LANES * j, LANES * (j + 1))
        q = q_ref[0, :, cols]
        kc = kc_ref[0, :, cols].astype(BF16)
        vc = vc_ref[0, :, cols].astype(BF16)
        kn = kn_ref[0, :, cols]
        vn = vn_ref[0, :, cols]
        outs = []
        for h in range(2):
            qh = jnp.where(low if h == 0 else ~low, q, jnp.zeros_like(q))
            sc = _dot_nt(qh, kc) + bc_ref[2 * j + h]
            sn = _dot_nt(qh, kn) + bn_ref[2 * j + h]
            m = jnp.maximum(jnp.max(sc, -1, keepdims=True), jnp.max(sn, -1, keepdims=True))
            pc = jnp.exp(sc - m)
            pn = jnp.exp(sn - m)
            l = jnp.sum(pc, -1, keepdims=True) + jnp.sum(pn, -1, keepdims=True)
            outs.append((_dot(pc.astype(BF16), vc) + _dot(pn.astype(BF16), vn)) / l)
        o_ref[0, :, cols] = jnp.where(low, outs[0], outs[1]).astype(o_ref.dtype)


def _band_sample_attn(q, kc, vc, kn, vn, bias_c, bias_n):
    b, t, _ = q.shape
    blk = lambda a: pl.BlockSpec((1,) + a.shape[1:], lambda i: (i, 0, 0))
    full = lambda a: pl.BlockSpec(a.shape, lambda i: (0, 0, 0))
    return pl.pallas_call(
        _band_sample_attn_kernel,
        grid=(b,),
        in_specs=[blk(q), blk(kc), blk(vc), blk(kn), blk(vn), full(bias_c), full(bias_n)],
        out_specs=pl.BlockSpec((1, t, D_MODEL), lambda i: (i, 0, 0)),
        out_shape=jax.ShapeDtypeStruct((b, t, D_MODEL), BF16),
        compiler_params=_params("parallel"),
        name="band_sample_attn",
    )(q, kc, vc, kn, vn, bias_c, bias_n)


def _band_bias(rel_table, q_pos, k_pos):
    nq, nk = len(q_pos), len(k_pos)
    assert (np.diff(q_pos) == 1).all() and (np.diff(k_pos) == 1).all()
    diag = int(q_pos[0] - k_pos[0]) + nq - 1 - np.arange(nq + nk - 1)
    u = rel_table.astype(F32)[:, np.clip(diag, -REL_MAX, REL_MAX) + REL_MAX]
    bias = jnp.stack([u[:, nq - 1 - q:nq - 1 - q + nk] for q in range(nq)], 1)
    vis = _static_visible(q_pos, k_pos, LEFT_CHUNKS)
    return jnp.where(jnp.asarray(vis)[None], bias, NEG_INF)


def _route(logits_t, rb_ref):
    scores = jax.nn.sigmoid(logits_t)
    rows = [scores[e:e + 1, :] for e in range(N_EXPERTS)]
    biased = [rows[e] + rb_ref[e] for e in range(N_EXPERTS)]
    best = None
    sel = None
    for g in range(N_GROUPS):
        a, b, c, d = biased[EXPERTS_PER_GROUP * g:EXPERTS_PER_GROUP * (g + 1)]
        top2 = jnp.maximum(jnp.maximum(jnp.maximum(a + b, a + c), jnp.maximum(a + d, b + c)),
                           jnp.maximum(b + d, c + d))
        if g == 0:
            best, sel = top2, jnp.zeros(top2.shape, jnp.int32)
        else:
            better = top2 > best
            sel = jnp.where(better, g, sel)
            best = jnp.where(better, top2, best)
    cand = [jnp.where(sel == e // EXPERTS_PER_GROUP, biased[e], NEG_INF) for e in range(N_EXPERTS)]

    def argmax_first(vals):
        top = functools.reduce(jnp.maximum, vals)
        idx = jnp.full(top.shape, N_EXPERTS, jnp.int32)
        for e in reversed(range(N_EXPERTS)):
            idx = jnp.where(vals[e] == top, e, idx)
        return idx

    i1 = argmax_first(cand)
    i2 = argmax_first([jnp.where(i1 == e, -jnp.inf, cand[e]) for e in range(N_EXPERTS)])
    picked = [jnp.where((i1 == e) | (i2 == e), rows[e], 0.0) for e in range(N_EXPERTS)]
    total = functools.reduce(jnp.add, picked)
    return [p / total for p in picked]


def _moe_kernel(x_ref, rw_ref, rb_ref, wg_ref, wu_ref, wd_ref, g_ref, b_ref, y_ref,
                xb_sc, comb_sc, acc_sc):
    grp = pl.program_id(1)

    @pl.when(grp == 0)
    def _():
        x = x_ref[...]
        xb = x.astype(BF16)
        xb_sc[...] = xb
        acc_sc[...] = DEEPNORM_ALPHA * x
        comb_rows = _route(_dot_nt(rw_ref[...], xb), rb_ref)
        comb = jnp.concatenate(comb_rows, axis=0).T
        for g in range(N_GROUPS):
            comb_sc[g] = comb[:, EXPERTS_PER_GROUP * g:EXPERTS_PER_GROUP * (g + 1)]

    xb = xb_sc[...]
    gate = _dot(xb, wg_ref[0])
    up = _dot(xb, wu_ref[0])
    comb = comb_sc[grp]
    parts = []
    for e in range(EXPERTS_PER_GROUP):
        cols = slice(D_EXPERT * e, D_EXPERT * (e + 1))
        h = jax.nn.silu(gate[:, cols]) * up[:, cols]
        parts.append((h * comb[:, e:e + 1]).astype(BF16))
    acc_sc[...] += _dot(jnp.concatenate(parts, axis=1), wd_ref[0])

    @pl.when(grp == N_GROUPS - 1)
    def _():
        y_ref[...] = _layer_norm(acc_sc[...], g_ref[...], b_ref[...])


def _moe_ln(x, rw_t, rb, wg, wu, wd, g, b):
    n = x.shape[0]
    tm = MOE_TILE
    row = pl.BlockSpec((tm, D_MODEL), lambda i, j: (i, 0))
    full = lambda a: pl.BlockSpec(a.shape, lambda i, j: (0, 0))
    grp = lambda a: pl.BlockSpec((1,) + a.shape[1:], lambda i, j: (j, 0, 0))
    return pl.pallas_call(
        _moe_kernel,
        grid=(n // tm, N_GROUPS),
        in_specs=[row, full(rw_t), pl.BlockSpec(memory_space=pltpu.SMEM),
                  grp(wg), grp(wu), grp(wd), full(g), full(b)],
        out_specs=row,
        out_shape=jax.ShapeDtypeStruct((n, D_MODEL), F32),
        scratch_shapes=[pltpu.VMEM((tm, D_MODEL), BF16),
                        pltpu.VMEM((N_GROUPS, tm, EXPERTS_PER_GROUP), F32),
                        pltpu.VMEM((tm, D_MODEL), F32)],
        compiler_params=_params("parallel", "arbitrary"),
        name="moe_ln",
    )(x, rw_t, rb, wg, wu, wd, g, b)


def _rope_tables():
    half = MLA_ROPE // 2
    inv = ROPE_THETA ** (-jnp.arange(half, dtype=F32) / half)
    pos = jnp.concatenate([jnp.arange(SEQ, dtype=jnp.int32),
                           jnp.tile(PAST_LEN + jnp.arange(DEC_SEQ, dtype=jnp.int32), DEC_BATCH)])
    ang = pos.astype(F32)[:, None] * inv[None, :]
    reps = LANES // half
    return jnp.tile(jnp.cos(ang), (1, reps)), jnp.tile(jnp.sin(ang), (1, reps))


def _rot_cols(w):
    half = MLA_ROPE // 2
    return jnp.concatenate([-w[..., half:], w[..., :half]], -1)


def _mla_weights(w_in, w_uq, w_uk, w_uv, w_o):
    o_kv = MLA_Q_LORA + MLA_KV_LORA
    w_r = w_in[:, o_kv:]
    zpad = jnp.zeros((D_MODEL, LANES - 2 * MLA_ROPE), F32)
    w_in_p = jnp.concatenate([w_in[:, :o_kv], w_r, w_r, zpad,
                              _rot_cols(w_r), _rot_cols(w_r), zpad], 1).astype(BF16)
    wq = w_uq.reshape(MLA_Q_LORA, N_PAIRS, 2, MLA_QK)
    nope = wq[..., :MLA_NOPE].reshape(MLA_Q_LORA, N_PAIRS, 2 * MLA_NOPE)
    ropew = wq[..., MLA_NOPE:]
    zq = jnp.zeros((MLA_Q_LORA, N_PAIRS, LANES - 2 * MLA_ROPE), F32)
    rope_cols = jnp.concatenate([ropew.reshape(MLA_Q_LORA, N_PAIRS, 2 * MLA_ROPE), zq], -1)
    rot_cols = jnp.concatenate([_rot_cols(ropew).reshape(MLA_Q_LORA, N_PAIRS, 2 * MLA_ROPE), zq], -1)
    wqa = jnp.concatenate([nope, rope_cols], -1).reshape(MLA_Q_LORA, 2 * LANES * N_PAIRS).astype(BF16)
    wqb = rot_cols.reshape(MLA_Q_LORA, LANES * N_PAIRS).astype(BF16)
    return w_in_p, wqa, wqb, w_uk.astype(BF16), w_uv.astype(BF16), w_o.astype(BF16)


def _group_experts(w_gate, w_up, w_down):
    d, f = D_MODEL, D_EXPERT
    up_layout = lambda w: (w.reshape(N_GROUPS, EXPERTS_PER_GROUP, d, f).transpose(0, 2, 1, 3)
                           .reshape(N_GROUPS, d, EXPERTS_PER_GROUP * f).astype(BF16))
    wd = w_down.reshape(N_GROUPS, EXPERTS_PER_GROUP * f, d).astype(BF16)
    return up_layout(w_gate), up_layout(w_up), wd


def kernel(x_prompt, x_sample, cache_mla_ckv, cache_mla_krope, cache_band_k, cache_band_v,
           mla_w_in, mla_q_norm, mla_kv_norm, mla_w_uq, mla_w_uk, mla_w_uv, mla_w_o,
           band_w_qkv, band_rel_bias, band_w_o,
           router_w, router_b, moe_w_gate, moe_w_up, moe_w_down,
           ln_mix_g, ln_mix_b, ln_ffn_g, ln_ffn_b):
    x = jnp.concatenate([x_prompt.reshape(SEQ, D_MODEL), x_sample.reshape(N_SAMPLE, D_MODEL)], 0)
    cos, sin = _rope_tables()
    rw_t = router_w.T.astype(BF16)
    rb = router_b.astype(F32)
    band_keep = cache_band_k.shape[2]
    row2 = lambda a: a.reshape(1, -1)

    ckv_p, kr_p, ckv_s, kr_s = [], [], [], []
    bk_p, bv_p, bk_s, bv_s = [], [], [], []
    for i in range(DEPTH):
        j = i // N_MIXERS
        if i % N_MIXERS == 0:
            w_in_p, wqa, wqb, wuk, wuv, w_o = _mla_weights(
                mla_w_in[j], mla_w_uq[j], mla_w_uk[j], mla_w_uv[j], mla_w_o[j])
            ckv, kr128, q = _mla_proj(x, w_in_p, row2(mla_q_norm[j]), row2(mla_kv_norm[j]),
                                      wqa, wqb, cos, sin)
            kr = kr128[:, :MLA_ROPE]
            ckv_p.append(ckv[:SEQ].reshape(1, SEQ, MLA_KV_LORA))
            kr_p.append(kr[:SEQ].reshape(1, SEQ, MLA_ROPE))
            ckv_s.append(ckv[SEQ:].reshape(DEC_BATCH, DEC_SEQ, MLA_KV_LORA))
            kr_s.append(kr[SEQ:].reshape(DEC_BATCH, DEC_SEQ, MLA_ROPE))
            k_p, vt_p = _kv_expand(ckv, kr128, wuk, wuv.T, SEQ, transpose_v=True)
            o_p = _mla_prompt_attn(q, k_p, vt_p)
            cache_kr = cache_mla_krope[j]
            cache_kr128 = jnp.concatenate(
                [cache_kr, cache_kr, jnp.zeros(cache_kr.shape[:2] + (LANES - 2 * MLA_ROPE,), F32)], -1)
            n_ctx = PAST_LEN + DEC_SEQ
            ckv_ctx = jnp.concatenate([cache_mla_ckv[j], ckv_s[-1]], 1).reshape(DEC_BATCH * n_ctx, -1)
            kr_ctx = jnp.concatenate([cache_kr128, kr128[SEQ:].reshape(DEC_BATCH, DEC_SEQ, LANES)], 1)
            k_s, v_s = _kv_expand(ckv_ctx, kr_ctx.reshape(DEC_BATCH * n_ctx, LANES), wuk, wuv,
                                  DEC_BATCH * n_ctx)
            o_s = _mla_sample_attn(q[SEQ:].reshape(DEC_BATCH, DEC_SEQ, -1),
                                   k_s.reshape(DEC_BATCH, n_ctx, -1), v_s.reshape(DEC_BATCH, n_ctx, -1))
            o = jnp.concatenate([o_p, o_s.reshape(N_SAMPLE, D_MODEL)], 0)
        else:
            n_keep = min(BAND_KEEP_MAX, SEQ) + N_SAMPLE
            q, k, v, k_f, v_f = _band_qkv(x, band_w_qkv[j].astype(BF16), n_keep)
            keep = n_keep - N_SAMPLE
            hd = (BAND_HEADS, BAND_HEAD_DIM)
            bk_p.append(k_f[:keep].reshape((1, keep) + hd))
            bv_p.append(v_f[:keep].reshape((1, keep) + hd))
            bk_s.append(k_f[keep:].reshape((DEC_BATCH, DEC_SEQ) + hd))
            bv_s.append(v_f[keep:].reshape((DEC_BATCH, DEC_SEQ) + hd))
            w_o = band_w_o[j].astype(BF16)
            pad = ((BAND_KEEP_MAX, 0), (0, 0))
            q_off = np.arange(BAND_QSUB)
            bias_p = _band_bias(band_rel_bias[j], BAND_KEEP_MAX + q_off, np.arange(BAND_WIN))
            o_p = _band_prompt_attn(q, jnp.pad(k[:SEQ], pad), jnp.pad(v[:SEQ], pad), bias_p)
            q_pos = PAST_LEN + np.arange(DEC_SEQ)
            pos_c = PAST_LEN - band_keep + np.arange(band_keep)
            bias_c = _band_bias(band_rel_bias[j], q_pos, pos_c)
            bias_n = _band_bias(band_rel_bias[j], q_pos, q_pos)
            s3 = lambda a: a[SEQ:].reshape(DEC_BATCH, DEC_SEQ, D_MODEL)
            o_s = _band_sample_attn(s3(q), cache_band_k[j].reshape(DEC_BATCH, band_keep, D_MODEL),
                                    cache_band_v[j].reshape(DEC_BATCH, band_keep, D_MODEL),
                                    s3(k), s3(v), bias_c, bias_n)
            o = jnp.concatenate([o_p, o_s.reshape(N_SAMPLE, D_MODEL)], 0)
        x = _out_ln(o, w_o, x, row2(ln_mix_g[i]), row2(ln_mix_b[i]))
        wg, wu, wd = _group_experts(moe_w_gate[i], moe_w_up[i], moe_w_down[i])
        x = _moe_ln(x, rw_t, rb, wg, wu, wd, row2(ln_ffn_g[i]), row2(ln_ffn_b[i]))

    return (x[:SEQ].reshape(1, SEQ, D_MODEL), x[SEQ:].reshape(DEC_BATCH, DEC_SEQ, D_MODEL),
            jnp.stack(ckv_p, 0), jnp.stack(kr_p, 0), jnp.stack(ckv_s, 0), jnp.stack(kr_s, 0),
            jnp.stack(bk_p, 0), jnp.stack(bv_p, 0), jnp.stack(bk_s, 0), jnp.stack(bv_s, 0))
```

```python
import functools
import math

import numpy as np
import jax
import jax.numpy as jnp
from jax import lax
from jax.experimental import pallas as pl
from jax.experimental.pallas import tpu as pltpu

D_MODEL = 1024
SEQ = 16384
DEPTH = 2
DEC_BATCH = 32
DEC_SEQ = 32
PAST_LEN = 1024
CHUNK = 64
N_MIXERS = 2

MLA_HEADS = 16
MLA_Q_LORA = 384
MLA_KV_LORA = 256
MLA_NOPE = 64
MLA_ROPE = 32
MLA_QK = MLA_NOPE + MLA_ROPE
MLA_V = 64
ROPE_THETA = 10000.0

BAND_HEADS = 16
BAND_HEAD_DIM = D_MODEL // BAND_HEADS
LEFT_CHUNKS = 8
BAND_KEEP_MAX = LEFT_CHUNKS * CHUNK
REL_MAX = 128

N_EXPERTS = 16
N_GROUPS = 4
EXPERTS_PER_GROUP = N_EXPERTS // N_GROUPS
D_EXPERT = 256

DEEPNORM_ALPHA = (2.0 * DEPTH) ** 0.25
NORM_EPS = 1e-5
NEG_INF = -1e30

N_SAMPLE = DEC_BATCH * DEC_SEQ
N_TOKENS = SEQ + N_SAMPLE
N_PAIRS = MLA_HEADS // 2
LANES = 128
VMEM_LIMIT = 56 * 1024 * 1024

TOKEN_TILE = 512
MOE_TILE = 512
ATTN_TQ = 512
ATTN_TK = 512
ONES_ROWS = 16
BAND_QSUB = 128
BAND_WIN = BAND_KEEP_MAX + BAND_QSUB
BAND_QBLOCK = 1024
BAND_GROUP = 4

BF16 = jnp.bfloat16
F32 = jnp.float32


def _params(*sem):
    return pltpu.CompilerParams(dimension_semantics=sem, vmem_limit_bytes=VMEM_LIMIT)


def _dot(a, b):
    return jnp.dot(a, b, preferred_element_type=F32)


def _dot_nt(a, b):
    return lax.dot_general(a, b, (((1,), (1,)), ((), ())), preferred_element_type=F32)


def _layer_norm(y, g, b):
    mu = jnp.mean(y, -1, keepdims=True)
    d = y - mu
    var = jnp.mean(d * d, -1, keepdims=True)
    return d * lax.rsqrt(var + NORM_EPS) * g + b


def _rms_norm(y, g):
    return y * lax.rsqrt(jnp.mean(y * y, -1, keepdims=True) + NORM_EPS) * g


def _mla_proj_kernel(x_ref, w_in_ref, qn_ref, kvn_ref, wqa_ref, wqb_ref, cos_ref, sin_ref,
                     ckv_ref, kr_ref, q_ref):
    xb = x_ref[...].astype(BF16)
    a = _dot(xb, w_in_ref[...])
    cq = _rms_norm(a[:, :MLA_Q_LORA], qn_ref[...])
    o_kv = MLA_Q_LORA + MLA_KV_LORA
    ckv_ref[...] = _rms_norm(a[:, MLA_Q_LORA:o_kv], kvn_ref[...])
    cos = cos_ref[...]
    sin = sin_ref[...]
    kr_ref[...] = a[:, o_kv:o_kv + LANES] * cos + a[:, o_kv + LANES:] * sin
    cqb = cq.astype(BF16)
    qa = _dot(cqb, wqa_ref[...])
    qb = _dot(cqb, wqb_ref[...])
    scale = math.log2(math.e) / math.sqrt(MLA_QK)
    for j in range(N_PAIRS):
        c0 = 2 * LANES * j
        q_ref[:, c0:c0 + LANES] = (qa[:, c0:c0 + LANES] * scale).astype(BF16)
        rot = qa[:, c0 + LANES:c0 + 2 * LANES] * cos + qb[:, LANES * j:LANES * (j + 1)] * sin
        q_ref[:, c0 + LANES:c0 + 2 * LANES] = (rot * scale).astype(BF16)


def _mla_proj(x, w_in_p, qn, kvn, wqa, wqb, cos, sin):
    n = x.shape[0]
    tm = TOKEN_TILE
    row = lambda w: pl.BlockSpec((tm, w), lambda i: (i, 0))
    full = lambda a: pl.BlockSpec(a.shape, lambda i: (0, 0))
    return pl.pallas_call(
        _mla_proj_kernel,
        grid=(n // tm,),
        in_specs=[row(D_MODEL), full(w_in_p), full(qn), full(kvn), full(wqa), full(wqb),
                  row(LANES), row(LANES)],
        out_specs=[row(MLA_KV_LORA), row(LANES), row(2 * LANES * N_PAIRS)],
        out_shape=[jax.ShapeDtypeStruct((n, MLA_KV_LORA), F32),
                   jax.ShapeDtypeStruct((n, LANES), F32),
                   jax.ShapeDtypeStruct((n, 2 * LANES * N_PAIRS), BF16)],
        compiler_params=_params("parallel"),
        name="mla_proj",
    )(x, w_in_p, qn, kvn, wqa, wqb, cos, sin)


def _kv_expand_kernel(ckv_ref, kr_ref, wuk_ref, wuv_ref, k_ref, v_ref, *, transpose_v):
    cb = ckv_ref[...].astype(BF16)
    kn = _dot(cb, wuk_ref[...])
    if transpose_v:
        v_ref[0] = _dot_nt(wuv_ref[...], cb).astype(BF16)
    else:
        v_ref[...] = _dot(cb, wuv_ref[...]).astype(BF16)
    krb = kr_ref[...].astype(BF16)
    for j in range(N_PAIRS):
        c0 = 2 * LANES * j
        k_ref[:, c0:c0 + LANES] = kn[:, LANES * j:LANES * (j + 1)].astype(BF16)
        k_ref[:, c0 + LANES:c0 + 2 * LANES] = krb


def _kv_expand(ckv, kr128, wuk, wuv, n, transpose_v=False):
    tm = ATTN_TK if transpose_v else TOKEN_TILE
    row = lambda w: pl.BlockSpec((tm, w), lambda i: (i, 0))
    full = lambda a: pl.BlockSpec(a.shape, lambda i: (0, 0))
    nv = LANES * N_PAIRS
    if transpose_v:
        v_spec = pl.BlockSpec((1, nv, tm), lambda i: (i, 0, 0))
        v_shape = jax.ShapeDtypeStruct((n // tm, nv, tm), BF16)
    else:
        v_spec = row(nv)
        v_shape = jax.ShapeDtypeStruct((n, nv), BF16)
    return pl.pallas_call(
        functools.partial(_kv_expand_kernel, transpose_v=transpose_v),
        grid=(n // tm,),
        in_specs=[row(MLA_KV_LORA), row(LANES), full(wuk), full(wuv)],
        out_specs=[row(2 * LANES * N_PAIRS), v_spec],
        out_shape=[jax.ShapeDtypeStruct((n, 2 * LANES * N_PAIRS), BF16), v_shape],
        compiler_params=_params("parallel"),
        name="mla_kv_expand_t" if transpose_v else "mla_kv_expand",
    )(ckv, kr128, wuk, wuv)


def _pair_query_masks(width):
    lane = lax.broadcasted_iota(jnp.int32, (1, width), 1)
    half = LANES // 2
    r0 = LANES
    m0 = (lane < half) | ((lane >= r0) & (lane < r0 + MLA_ROPE))
    m1 = ((lane >= half) & (lane < LANES)) | ((lane >= r0 + MLA_ROPE) & (lane < r0 + 2 * MLA_ROPE))
    return m0, m1


def _mla_prompt_attn_kernel(q_ref, k_ref, vt_ref, o_ref, qh_sc, s_sc, m_sc, acc_sc):
    i = pl.program_id(1)
    tq, tk = ATTN_TQ, ATTN_TK
    half = LANES // 2
    q = q_ref[...]
    for h, mask in enumerate(_pair_query_masks(2 * LANES)):
        qh_sc[h] = jnp.where(mask, q, jnp.zeros_like(q))
        m_sc[h] = jnp.full((1, tq), NEG_INF, F32)
        acc_sc[h] = jnp.zeros((half + ONES_ROWS, tq), F32)
    ones = jnp.ones((ONES_ROWS, tk), BF16)

    def scores(kb, h, slot):
        k = k_ref[pl.ds(pl.multiple_of(kb * tk, tk), tk), :]
        s_sc[h, slot] = _dot_nt(k, qh_sc[h])

    def update(kb, h, slot, visible=None):
        s = s_sc[h, slot]
        if visible is not None:
            s = jnp.where(visible, s, NEG_INF)
        m_prev = m_sc[h]
        m_new = jnp.maximum(m_prev, jnp.max(s, 0, keepdims=True))
        alpha = jnp.exp2(m_prev - m_new)
        p = jnp.exp2(s - m_new).astype(BF16)
        vt = jnp.concatenate([vt_ref[kb, half * h:half * (h + 1), :], ones], axis=0)
        acc_sc[h] = alpha * acc_sc[h] + _dot(vt, p)
        m_sc[h] = m_new

    scores(0, 0, 0)
    scores(0, 1, 0)

    def advance(kb, slot):
        for h in range(2):
            scores(kb + 1, h, 1 - slot)
            update(kb, h, slot)

    def body(kk, c):
        advance(2 * kk, 0)
        advance(2 * kk + 1, 1)
        return c

    lax.fori_loop(0, i // 2, body, 0)

    def diagonal(slot):
        k_chunk = lax.broadcasted_iota(jnp.int32, (tk, tq), 0) // CHUNK
        q_chunk = lax.broadcasted_iota(jnp.int32, (tk, tq), 1) // CHUNK
        for h in range(2):
            update(i, h, slot, k_chunk <= q_chunk)

    @pl.when(i % 2 == 0)
    def _():
        diagonal(0)

    @pl.when(i % 2 == 1)
    def _():
        advance(i - 1, 0)
        diagonal(1)

    o_t = jnp.concatenate([acc_sc[h, :half] / acc_sc[h, half:half + 1] for h in range(2)], axis=0)
    o_ref[...] = o_t.T.astype(o_ref.dtype)


def _mla_prompt_attn(q, k, vt):
    s = k.shape[0]
    tq = ATTN_TQ
    assert ATTN_TQ == ATTN_TK and vt.shape == (s // ATTN_TK, LANES * N_PAIRS, ATTN_TK)
    return pl.pallas_call(
        _mla_prompt_attn_kernel,
        grid=(N_PAIRS, s // tq),
        in_specs=[pl.BlockSpec((tq, 2 * LANES), lambda j, i: (i, j)),
                  pl.BlockSpec((s, 2 * LANES), lambda j, i: (0, j)),
                  pl.BlockSpec((s // ATTN_TK, LANES, ATTN_TK), lambda j, i: (0, j, 0))],
        out_specs=pl.BlockSpec((tq, LANES), lambda j, i: (i, j)),
        out_shape=jax.ShapeDtypeStruct((s, LANES * N_PAIRS), BF16),
        scratch_shapes=[pltpu.VMEM((2, tq, 2 * LANES), BF16),
                        pltpu.VMEM((2, 2, ATTN_TK, tq), F32),
                        pltpu.VMEM((2, 1, tq), F32),
                        pltpu.VMEM((2, LANES // 2 + ONES_ROWS, tq), F32)],
        compiler_params=_params("parallel", "arbitrary"),
        name="mla_prompt_attn",
    )(q, k, vt)


def _static_visible(q_pos, k_pos, left_chunks=None):
    qc = (q_pos // CHUNK)[:, None]
    kc = (k_pos // CHUNK)[None, :]
    vis = kc <= qc
    if left_chunks is not None:
        vis = vis & (kc >= qc - left_chunks) & (k_pos[None, :] >= 0)
    return vis


def _mla_sample_attn_kernel(q_ref, k_ref, v_ref, o_ref, *, visible):
    masks = _pair_query_masks(2 * LANES)
    lane = lax.broadcasted_iota(jnp.int32, (1, LANES), 1)
    vis = None if visible is None else jnp.asarray(visible)
    for j in range(N_PAIRS):
        q = q_ref[0, :, 2 * LANES * j:2 * LANES * (j + 1)]
        k = k_ref[0, :, 2 * LANES * j:2 * LANES * (j + 1)]
        v = v_ref[0, :, LANES * j:LANES * (j + 1)]
        outs = []
        for h in range(2):
            s = _dot_nt(jnp.where(masks[h], q, jnp.zeros_like(q)), k)
            if vis is not None:
                s = jnp.where(vis, s, NEG_INF)
            m = jnp.max(s, -1, keepdims=True)
            p = jnp.exp2(s - m)
            l = jnp.sum(p, -1, keepdims=True)
            outs.append(_dot(p.astype(BF16), v) / l)
        o_ref[0, :, LANES * j:LANES * (j + 1)] = jnp.where(
            lane < LANES // 2, outs[0], outs[1]).astype(o_ref.dtype)


def _mla_sample_attn(q, k, v):
    b, t, _ = q.shape
    nk = k.shape[1]
    vis = _static_visible(PAST_LEN + np.arange(t), np.arange(nk))
    visible = None if vis.all() else vis
    blk = lambda a: pl.BlockSpec((1,) + a.shape[1:], lambda i: (i, 0, 0))
    return pl.pallas_call(
        functools.partial(_mla_sample_attn_kernel, visible=visible),
        grid=(b,),
        in_specs=[blk(q), blk(k), blk(v)],
        out_specs=pl.BlockSpec((1, t, LANES * N_PAIRS), lambda i: (i, 0, 0)),
        out_shape=jax.ShapeDtypeStruct((b, t, LANES * N_PAIRS), BF16),
        compiler_params=_params("parallel"),
        name="mla_sample_attn",
    )(q, k, v)


def _out_ln_kernel(o_ref, w_ref, x_ref, g_ref, b_ref, y_ref):
    y = DEEPNORM_ALPHA * x_ref[...] + _dot(o_ref[...], w_ref[...])
    y_ref[...] = _layer_norm(y, g_ref[...], b_ref[...])


def _out_ln(o, w_o, x, g, b):
    n = x.shape[0]
    tm = TOKEN_TILE
    row = pl.BlockSpec((tm, D_MODEL), lambda i: (i, 0))
    full = lambda a: pl.BlockSpec(a.shape, lambda i: (0, 0))
    return pl.pallas_call(
        _out_ln_kernel,
        grid=(n // tm,),
        in_specs=[row, full(w_o), row, full(g), full(b)],
        out_specs=row,
        out_shape=jax.ShapeDtypeStruct((n, D_MODEL), F32),
        compiler_params=_params("parallel"),
        name="out_proj_ln",
    )(o, w_o, x, g, b)


def _band_qkv_kernel(x_ref, wqk_ref, wv_ref, wvt_ref, q_ref, k_ref, vt_ref, kf_ref, vf_ref, *,
                     first_keep_tile):
    xb = x_ref[...].astype(BF16)
    qk = _dot(xb, wqk_ref[...])
    scale = math.log2(math.e) / math.sqrt(BAND_HEAD_DIM)
    q_ref[...] = (qk[:, :D_MODEL] * scale).astype(BF16)
    k = qk[:, D_MODEL:]
    k_ref[...] = k.astype(BF16)
    vt = _dot_nt(wvt_ref[...], xb).astype(BF16)
    for t in range(vt_ref.shape[0]):
        vt_ref[t] = vt[:, LANES * t:LANES * (t + 1)]

    @pl.when(pl.program_id(0) >= first_keep_tile)
    def _():
        kf_ref[...] = k
        vf_ref[...] = _dot(xb, wv_ref[...])


def _band_qkv(x, w_qkv, n_keep):
    n = x.shape[0]
    tm = TOKEN_TILE
    first = (n - n_keep) // tm
    w = w_qkv.astype(BF16)
    wqk, wv = w[:, :2 * D_MODEL], w[:, 2 * D_MODEL:]
    row = pl.BlockSpec((tm, D_MODEL), lambda i: (i, 0))
    keep = pl.BlockSpec((tm, D_MODEL), lambda i: (jnp.maximum(i - first, 0), 0))
    full = lambda a: pl.BlockSpec(a.shape, lambda i: (0, 0))
    bf = jax.ShapeDtypeStruct((n, D_MODEL), BF16)
    kf = jax.ShapeDtypeStruct((n_keep, D_MODEL), F32)
    return pl.pallas_call(
        functools.partial(_band_qkv_kernel, first_keep_tile=first),
        grid=(n // tm,),
        in_specs=[row, full(wqk), full(wv), full(wv)],
        out_specs=[row, row, pl.BlockSpec((tm // LANES, D_MODEL, LANES), lambda i: (i, 0, 0)),
                   keep, keep],
        out_shape=[bf, bf, jax.ShapeDtypeStruct((n // LANES, D_MODEL, LANES), BF16), kf, kf],
        compiler_params=_params("arbitrary"),
        name="band_qkv",
    )(x, wqk, wv, wv.T)


def _band_prompt_attn_kernel(q_ref, k_ref, vt_ref, bias_ref, o_ref):
    i = pl.program_id(1)
    half = LANES // 2
    low = lax.broadcasted_iota(jnp.int32, (1, LANES), 1) < half
    n_win = BAND_WIN // LANES
    n_left = BAND_KEEP_MAX // LANES

    def body(g, c):
        work = []
        for u in range(BAND_GROUP):
            sb = g * BAND_GROUP + u
            blk = i * (BAND_QBLOCK // BAND_QSUB) + sb
            b0 = jnp.maximum(blk - n_left, 0)
            rows = pl.ds(pl.multiple_of(sb * BAND_QSUB, BAND_QSUB), BAND_QSUB)
            q = q_ref[rows, :]
            zero = jnp.zeros_like(q)
            qq = jnp.concatenate([jnp.where(low, q, zero), jnp.where(low, zero, q)], axis=0)
            kw = k_ref[pl.ds(pl.multiple_of(b0 * LANES, LANES), BAND_WIN), :]
            s = _dot_nt(kw, qq) + bias_ref[0, jnp.minimum(blk, n_left)]
            work.append((rows, b0, s))
        for rows, b0, s in work:
            m = jnp.max(s, 0, keepdims=True)
            p = jnp.exp2(s - m)
            l = jnp.sum(p, 0, keepdims=True)
            pb = p.astype(BF16)
            vt = jnp.concatenate([vt_ref[b0 + w] for w in range(n_win)], axis=1)
            o_t = jnp.concatenate(
                [_dot(vt[half * h:half * (h + 1)], pb[:, LANES * h:LANES * (h + 1)])
                 / l[:, LANES * h:LANES * (h + 1)] for h in range(2)], axis=0)
            o_ref[rows, :] = o_t.T.astype(o_ref.dtype)
        return c

    lax.fori_loop(0, BAND_QBLOCK // BAND_QSUB // BAND_GROUP, body, 0)


def _band_prompt_attn(q, k, vt, bias, s):
    return pl.pallas_call(
        _band_prompt_attn_kernel,
        grid=(N_PAIRS, s // BAND_QBLOCK),
        in_specs=[pl.BlockSpec((BAND_QBLOCK, LANES), lambda j, i: (i, j)),
                  pl.BlockSpec((s, LANES), lambda j, i: (0, j)),
                  pl.BlockSpec((s // LANES, LANES, LANES), lambda j, i: (0, j, 0)),
                  pl.BlockSpec((1,) + bias.shape[1:], lambda j, i: (j, 0, 0, 0))],
        out_specs=pl.BlockSpec((BAND_QBLOCK, LANES), lambda j, i: (i, j)),
        out_shape=jax.ShapeDtypeStruct((s, D_MODEL), BF16),
        compiler_params=_params("parallel", "arbitrary"),
        name="band_prompt_attn",
    )(q, k, vt, bias)


def _band_sample_attn_kernel(q_ref, kc_ref, vc_ref, kn_ref, vn_ref, bc_ref, bn_ref, o_ref):
    lane = lax.broadcasted_iota(jnp.int32, (1, LANES), 1)
    low = lane < LANES // 2
    for j in range(N_PAIRS):
        cols = slice(LANES * j, LANES * (j + 1))
        q = q_ref[0, :, cols]
        kc = kc_ref[0, :, cols].astype(BF16)
        vc = vc_ref[0, :, cols].astype(BF16)
        kn = kn_ref[0, :, cols].astype(BF16)
        vn = vn_ref[0, :, cols].astype(BF16)
        outs = []
        for h in range(2):
            qh = jnp.where(low if h == 0 else ~low, q, jnp.zeros_like(q))
            sc = _dot_nt(qh, kc) + bc_ref[2 * j + h]
            sn = _dot_nt(qh, kn) + bn_ref[2 * j + h]
            m = jnp.maximum(jnp.max(sc, -1, keepdims=True), jnp.max(sn, -1, keepdims=True))
            pc = jnp.exp2(sc - m)
            pn = jnp.exp2(sn - m)
            l = jnp.sum(pc, -1, keepdims=True) + jnp.sum(pn, -1, keepdims=True)
            outs.append((_dot(pc.astype(BF16), vc) + _dot(pn.astype(BF16), vn)) / l)
        o_ref[0, :, cols] = jnp.where(low, outs[0], outs[1]).astype(o_ref.dtype)


def _band_sample_attn(q, kc, vc, kn, vn, bias_c, bias_n):
    b, t, _ = q.shape
    blk = lambda a: pl.BlockSpec((1,) + a.shape[1:], lambda i: (i, 0, 0))
    full = lambda a: pl.BlockSpec(a.shape, lambda i: (0, 0, 0))
    return pl.pallas_call(
        _band_sample_attn_kernel,
        grid=(b,),
        in_specs=[blk(q), blk(kc), blk(vc), blk(kn), blk(vn), full(bias_c), full(bias_n)],
        out_specs=pl.BlockSpec((1, t, D_MODEL), lambda i: (i, 0, 0)),
        out_shape=jax.ShapeDtypeStruct((b, t, D_MODEL), BF16),
        compiler_params=_params("parallel"),
        name="band_sample_attn",
    )(q, kc, vc, kn, vn, bias_c, bias_n)


def _band_bias(rel_table, q_pos, k_pos):
    nq, nk = len(q_pos), len(k_pos)
    assert (np.diff(q_pos) == 1).all() and (np.diff(k_pos) == 1).all()
    span = nq + nk - 1
    diag = int(q_pos[0] - k_pos[0]) + nq - 1 - np.arange(span)
    u = rel_table.astype(F32)[:, np.clip(diag, -REL_MAX, REL_MAX) + REL_MAX] * math.log2(math.e)
    u = jnp.pad(u, ((0, 0), (0, 1)))
    skew = jnp.tile(u, (1, nq))[:, :nq * span].reshape(-1, nq, span)
    bias = skew[:, :, nq - 1:nq - 1 + nk]
    vis = _static_visible(q_pos, k_pos, LEFT_CHUNKS)
    return jnp.where(jnp.asarray(vis)[None], bias, NEG_INF)


def _route(logits_t, rb_ref):
    scores = jax.nn.sigmoid(logits_t)
    rows = [scores[e:e + 1, :] for e in range(N_EXPERTS)]
    biased = [rows[e] + rb_ref[e] for e in range(N_EXPERTS)]
    best = None
    sel = None
    for g in range(N_GROUPS):
        a, b, c, d = biased[EXPERTS_PER_GROUP * g:EXPERTS_PER_GROUP * (g + 1)]
        top2 = jnp.maximum(jnp.maximum(jnp.maximum(a + b, a + c), jnp.maximum(a + d, b + c)),
                           jnp.maximum(b + d, c + d))
        if g == 0:
            best, sel = top2, jnp.zeros(top2.shape, jnp.int32)
        else:
            better = top2 > best
            sel = jnp.where(better, g, sel)
            best = jnp.where(better, top2, best)
    cand = [jnp.where(sel == e // EXPERTS_PER_GROUP, biased[e], NEG_INF) for e in range(N_EXPERTS)]

    def argmax_first(vals):
        top = functools.reduce(jnp.maximum, vals)
        idx = jnp.full(top.shape, N_EXPERTS, jnp.int32)
        for e in reversed(range(N_EXPERTS)):
            idx = jnp.where(vals[e] == top, e, idx)
        return idx

    i1 = argmax_first(cand)
    i2 = argmax_first([jnp.where(i1 == e, -jnp.inf, cand[e]) for e in range(N_EXPERTS)])
    picked = [jnp.where((i1 == e) | (i2 == e), rows[e], 0.0) for e in range(N_EXPERTS)]
    total = functools.reduce(jnp.add, picked)
    return [p / total for p in picked]


def _moe_kernel(x_ref, rw_ref, rb_ref, wg_ref, wu_ref, wd_ref, g_ref, b_ref, y_ref,
                xb_sc, comb_sc, acc_sc):
    grp = pl.program_id(1)

    @pl.when(grp == 0)
    def _():
        x = x_ref[...]
        xb = x.astype(BF16)
        xb_sc[...] = xb
        acc_sc[...] = DEEPNORM_ALPHA * x
        comb_rows = _route(_dot_nt(rw_ref[...], xb), rb_ref)
        comb = jnp.concatenate(comb_rows, axis=0).T
        for g in range(N_GROUPS):
            comb_sc[g] = comb[:, EXPERTS_PER_GROUP * g:EXPERTS_PER_GROUP * (g + 1)]

    xb = xb_sc[...]
    comb = comb_sc[grp]
    parts = []
    for e in range(EXPERTS_PER_GROUP):
        h = jax.nn.silu(_dot(xb, wg_ref[e])) * _dot(xb, wu_ref[e])
        parts.append((h * comb[:, e:e + 1]).astype(BF16))
    acc_sc[...] += _dot(jnp.concatenate(parts, axis=1), wd_ref[0])

    @pl.when(grp == N_GROUPS - 1)
    def _():
        y_ref[...] = _layer_norm(acc_sc[...], g_ref[...], b_ref[...])


def _moe_ln(x, rw_t, rb, wg, wu, wd, g, b):
    n = x.shape[0]
    tm = MOE_TILE
    row = pl.BlockSpec((tm, D_MODEL), lambda i, j: (i, 0))
    full = lambda a: pl.BlockSpec(a.shape, lambda i, j: (0, 0))
    grp = lambda a: pl.BlockSpec((a.shape[0] // N_GROUPS,) + a.shape[1:], lambda i, j: (j, 0, 0))
    return pl.pallas_call(
        _moe_kernel,
        grid=(n // tm, N_GROUPS),
        in_specs=[row, full(rw_t), pl.BlockSpec(memory_space=pltpu.SMEM),
                  grp(wg), grp(wu), grp(wd), full(g), full(b)],
        out_specs=row,
        out_shape=jax.ShapeDtypeStruct((n, D_MODEL), F32),
        scratch_shapes=[pltpu.VMEM((tm, D_MODEL), BF16),
                        pltpu.VMEM((N_GROUPS, tm, EXPERTS_PER_GROUP), F32),
                        pltpu.VMEM((tm, D_MODEL), F32)],
        compiler_params=_params("parallel", "arbitrary"),
        name="moe_ln",
    )(x, rw_t, rb, wg, wu, wd, g, b)


def _rope_tables():
    half = MLA_ROPE // 2
    inv = ROPE_THETA ** (-jnp.arange(half, dtype=F32) / half)
    pos = jnp.concatenate([jnp.arange(SEQ, dtype=jnp.int32),
                           jnp.tile(PAST_LEN + jnp.arange(DEC_SEQ, dtype=jnp.int32), DEC_BATCH)])
    ang = pos.astype(F32)[:, None] * inv[None, :]
    reps = LANES // half
    return jnp.tile(jnp.cos(ang), (1, reps)), jnp.tile(jnp.sin(ang), (1, reps))


def _rot_cols(w):
    half = MLA_ROPE // 2
    return jnp.concatenate([-w[..., half:], w[..., :half]], -1)


def _mla_weights(w_in, w_uq, w_uk, w_uv, w_o):
    o_kv = MLA_Q_LORA + MLA_KV_LORA
    w_r = w_in[:, o_kv:]
    zpad = jnp.zeros((D_MODEL, LANES - 2 * MLA_ROPE), F32)
    w_in_p = jnp.concatenate([w_in[:, :o_kv], w_r, w_r, zpad,
                              _rot_cols(w_r), _rot_cols(w_r), zpad], 1).astype(BF16)
    wq = w_uq.reshape(MLA_Q_LORA, N_PAIRS, 2, MLA_QK)
    nope = wq[..., :MLA_NOPE].reshape(MLA_Q_LORA, N_PAIRS, 2 * MLA_NOPE)
    ropew = wq[..., MLA_NOPE:]
    zq = jnp.zeros((MLA_Q_LORA, N_PAIRS, LANES - 2 * MLA_ROPE), F32)
    rope_cols = jnp.concatenate([ropew.reshape(MLA_Q_LORA, N_PAIRS, 2 * MLA_ROPE), zq], -1)
    rot_cols = jnp.concatenate([_rot_cols(ropew).reshape(MLA_Q_LORA, N_PAIRS, 2 * MLA_ROPE), zq], -1)
    wqa = jnp.concatenate([nope, rope_cols], -1).reshape(MLA_Q_LORA, 2 * LANES * N_PAIRS).astype(BF16)
    wqb = rot_cols.reshape(MLA_Q_LORA, LANES * N_PAIRS).astype(BF16)
    return w_in_p, wqa, wqb, w_uk.astype(BF16), w_uv.astype(BF16), w_o.astype(BF16)


def _group_experts(w_gate, w_up, w_down):
    wd = w_down.reshape(N_GROUPS, EXPERTS_PER_GROUP * D_EXPERT, D_MODEL).astype(BF16)
    return w_gate.astype(BF16), w_up.astype(BF16), wd


def kernel(x_prompt, x_sample, cache_mla_ckv, cache_mla_krope, cache_band_k, cache_band_v,
           mla_w_in, mla_q_norm, mla_kv_norm, mla_w_uq, mla_w_uk, mla_w_uv, mla_w_o,
           band_w_qkv, band_rel_bias, band_w_o,
           router_w, router_b, moe_w_gate, moe_w_up, moe_w_down,
           ln_mix_g, ln_mix_b, ln_ffn_g, ln_ffn_b):
    x = jnp.concatenate([x_prompt.reshape(SEQ, D_MODEL), x_sample.reshape(N_SAMPLE, D_MODEL)], 0)
    cos, sin = _rope_tables()
    rw_t = router_w.T.astype(BF16)
    rb = router_b.astype(F32)
    band_keep = cache_band_k.shape[2]
    row2 = lambda a: a.reshape(1, -1)

    ckv_p, kr_p, ckv_s, kr_s = [], [], [], []
    bk_p, bv_p, bk_s, bv_s = [], [], [], []
    for i in range(DEPTH):
        j = i // N_MIXERS
        if i % N_MIXERS == 0:
            w_in_p, wqa, wqb, wuk, wuv, w_o = _mla_weights(
                mla_w_in[j], mla_w_uq[j], mla_w_uk[j], mla_w_uv[j], mla_w_o[j])
            ckv, kr128, q = _mla_proj(x, w_in_p, row2(mla_q_norm[j]), row2(mla_kv_norm[j]),
                                      wqa, wqb, cos, sin)
            kr = kr128[:, :MLA_ROPE]
            ckv_p.append(ckv[:SEQ].reshape(1, SEQ, MLA_KV_LORA))
            kr_p.append(kr[:SEQ].reshape(1, SEQ, MLA_ROPE))
            ckv_s.append(ckv[SEQ:].reshape(DEC_BATCH, DEC_SEQ, MLA_KV_LORA))
            kr_s.append(kr[SEQ:].reshape(DEC_BATCH, DEC_SEQ, MLA_ROPE))
            k_p, vt_p = _kv_expand(ckv, kr128, wuk, wuv.T, SEQ, transpose_v=True)
            o_p = _mla_prompt_attn(q, k_p, vt_p)
            cache_kr = cache_mla_krope[j]
            cache_kr128 = jnp.concatenate(
                [cache_kr, cache_kr, jnp.zeros(cache_kr.shape[:2] + (LANES - 2 * MLA_ROPE,), F32)], -1)
            n_ctx = PAST_LEN + DEC_SEQ
            ckv_ctx = jnp.concatenate([cache_mla_ckv[j], ckv_s[-1]], 1).reshape(DEC_BATCH * n_ctx, -1)
            kr_ctx = jnp.concatenate([cache_kr128, kr128[SEQ:].reshape(DEC_BATCH, DEC_SEQ, LANES)], 1)
            k_s, v_s = _kv_expand(ckv_ctx, kr_ctx.reshape(DEC_BATCH * n_ctx, LANES), wuk, wuv,
                                  DEC_BATCH * n_ctx)
            o_s = _mla_sample_attn(q[SEQ:].reshape(DEC_BATCH, DEC_SEQ, -1),
                                   k_s.reshape(DEC_BATCH, n_ctx, -1), v_s.reshape(DEC_BATCH, n_ctx, -1))
            o = jnp.concatenate([o_p, o_s.reshape(N_SAMPLE, D_MODEL)], 0)
        else:
            n_keep = min(BAND_KEEP_MAX, SEQ) + N_SAMPLE
            q, k, vt, k_f, v_f = _band_qkv(x, band_w_qkv[j], n_keep)
            keep = n_keep - N_SAMPLE
            hd = (BAND_HEADS, BAND_HEAD_DIM)
            bk_p.append(k_f[:keep].reshape((1, keep) + hd))
            bv_p.append(v_f[:keep].reshape((1, keep) + hd))
            bk_s.append(k_f[keep:].reshape((DEC_BATCH, DEC_SEQ) + hd))
            bv_s.append(v_f[keep:].reshape((DEC_BATCH, DEC_SEQ) + hd))
            w_o = band_w_o[j].astype(BF16)
            q_off = np.arange(BAND_QSUB)
            starts = [BAND_QSUB * t for t in range(BAND_KEEP_MAX // BAND_QSUB)] + [BAND_KEEP_MAX]
            tiles = jnp.stack([_band_bias(band_rel_bias[j], st + q_off, np.arange(BAND_WIN))
                               for st in starts], 1)
            bias_p = (tiles.reshape(N_PAIRS, 2, len(starts), BAND_QSUB, BAND_WIN)
                      .transpose(0, 2, 4, 1, 3).reshape(N_PAIRS, len(starts), BAND_WIN, 2 * BAND_QSUB))
            o_p = _band_prompt_attn(q, k, vt, bias_p, SEQ)
            q_pos = PAST_LEN + np.arange(DEC_SEQ)
            pos_c = PAST_LEN - band_keep + np.arange(band_keep)
            bias_c = _band_bias(band_rel_bias[j], q_pos, pos_c)
            bias_n = _band_bias(band_rel_bias[j], q_pos, q_pos)
            s3 = lambda a: a.reshape(DEC_BATCH, DEC_SEQ, D_MODEL)
            o_s = _band_sample_attn(s3(q[SEQ:]), cache_band_k[j].reshape(DEC_BATCH, band_keep, D_MODEL),
                                    cache_band_v[j].reshape(DEC_BATCH, band_keep, D_MODEL),
                                    s3(k_f[keep:]), s3(v_f[keep:]), bias_c, bias_n)
            o = jnp.concatenate([o_p, o_s.reshape(N_SAMPLE, D_MODEL)], 0)
        x = _out_ln(o, w_o, x, row2(ln_mix_g[i]), row2(ln_mix_b[i]))
        wg, wu, wd = _group_experts(moe_w_gate[i], moe_w_up[i], moe_w_down[i])
        x = _moe_ln(x, rw_t, rb, wg, wu, wd, row2(ln_ffn_g[i]), row2(ln_ffn_b[i]))

    return (x[:SEQ].reshape(1, SEQ, D_MODEL), x[SEQ:].reshape(DEC_BATCH, DEC_SEQ, D_MODEL),
            jnp.stack(ckv_p, 0), jnp.stack(kr_p, 0), jnp.stack(ckv_s, 0), jnp.stack(kr_s, 0),
            jnp.stack(bk_p, 0), jnp.stack(bv_p, 0), jnp.stack(bk_s, 0), jnp.stack(bv_s, 0))
```

```python
import functools
import math

import numpy as np
import jax
import jax.numpy as jnp
from jax import lax
from jax.experimental import pallas as pl
from jax.experimental.pallas import tpu as pltpu

D_MODEL = 1024
SEQ = 16384
DEPTH = 2
DEC_BATCH = 32
DEC_SEQ = 32
PAST_LEN = 1024
CHUNK = 64
N_MIXERS = 2

MLA_HEADS = 16
MLA_Q_LORA = 384
MLA_KV_LORA = 256
MLA_NOPE = 64
MLA_ROPE = 32
MLA_QK = MLA_NOPE + MLA_ROPE
MLA_V = 64
ROPE_THETA = 10000.0

BAND_HEADS = 16
BAND_HEAD_DIM = D_MODEL // BAND_HEADS
LEFT_CHUNKS = 8
BAND_KEEP_MAX = LEFT_CHUNKS * CHUNK
REL_MAX = 128

N_EXPERTS = 16
N_GROUPS = 4
EXPERTS_PER_GROUP = N_EXPERTS // N_GROUPS
D_EXPERT = 256

DEEPNORM_ALPHA = (2.0 * DEPTH) ** 0.25
NORM_EPS = 1e-5
NEG_INF = -1e30

N_SAMPLE = DEC_BATCH * DEC_SEQ
N_TOKENS = SEQ + N_SAMPLE
N_PAIRS = MLA_HEADS // 2
LANES = 128
VMEM_LIMIT = 56 * 1024 * 1024

TOKEN_TILE = 512
MOE_TILE = 1024
ATTN_TQ = 512
ATTN_TK = 512
ONES_ROWS = 16
BAND_QSUB = 128
BAND_WIN = BAND_KEEP_MAX + BAND_QSUB
BAND_QBLOCK = 1024
BAND_GROUP = 4

BF16 = jnp.bfloat16
F32 = jnp.float32


def _params(*sem):
    return pltpu.CompilerParams(dimension_semantics=sem, vmem_limit_bytes=VMEM_LIMIT)


def _dot(a, b):
    return jnp.dot(a, b, preferred_element_type=F32)


def _dot_nt(a, b):
    return lax.dot_general(a, b, (((1,), (1,)), ((), ())), preferred_element_type=F32)


def _layer_norm(y, g, b):
    mu = jnp.mean(y, -1, keepdims=True)
    d = y - mu
    var = jnp.mean(d * d, -1, keepdims=True)
    return d * lax.rsqrt(var + NORM_EPS) * g + b


def _rms_norm(y, g):
    return y * lax.rsqrt(jnp.mean(y * y, -1, keepdims=True) + NORM_EPS) * g


def _mla_proj_kernel(x_ref, w_in_ref, qn_ref, kvn_ref, wqa_ref, wqb_ref, cos_ref, sin_ref,
                     ckv_ref, kr_ref, q_ref):
    xb = x_ref[...].astype(BF16)
    a = _dot(xb, w_in_ref[...])
    cq = _rms_norm(a[:, :MLA_Q_LORA], qn_ref[...])
    o_kv = MLA_Q_LORA + MLA_KV_LORA
    ckv_ref[...] = _rms_norm(a[:, MLA_Q_LORA:o_kv], kvn_ref[...])
    cos = cos_ref[...]
    sin = sin_ref[...]
    kr_ref[...] = a[:, o_kv:o_kv + LANES] * cos + a[:, o_kv + LANES:] * sin
    cqb = cq.astype(BF16)
    qa = _dot(cqb, wqa_ref[...])
    qb = _dot(cqb, wqb_ref[...])
    scale = math.log2(math.e) / math.sqrt(MLA_QK)
    for j in range(N_PAIRS):
        c0 = 2 * LANES * j
        q_ref[:, c0:c0 + LANES] = (qa[:, c0:c0 + LANES] * scale).astype(BF16)
        rot = qa[:, c0 + LANES:c0 + 2 * LANES] * cos + qb[:, LANES * j:LANES * (j + 1)] * sin
        q_ref[:, c0 + LANES:c0 + 2 * LANES] = (rot * scale).astype(BF16)


def _mla_proj(x, w_in_p, qn, kvn, wqa, wqb, cos, sin):
    n = x.shape[0]
    tm = TOKEN_TILE
    row = lambda w: pl.BlockSpec((tm, w), lambda i: (i, 0))
    full = lambda a: pl.BlockSpec(a.shape, lambda i: (0, 0))
    return pl.pallas_call(
        _mla_proj_kernel,
        grid=(n // tm,),
        in_specs=[row(D_MODEL), full(w_in_p), full(qn), full(kvn), full(wqa), full(wqb),
                  row(LANES), row(LANES)],
        out_specs=[row(MLA_KV_LORA), row(LANES), row(2 * LANES * N_PAIRS)],
        out_shape=[jax.ShapeDtypeStruct((n, MLA_KV_LORA), F32),
                   jax.ShapeDtypeStruct((n, LANES), F32),
                   jax.ShapeDtypeStruct((n, 2 * LANES * N_PAIRS), BF16)],
        compiler_params=_params("parallel"),
        name="mla_proj",
    )(x, w_in_p, qn, kvn, wqa, wqb, cos, sin)


def _kv_expand_kernel(ckv_ref, kr_ref, wuk_ref, wuvt_ref, k_ref, vt_ref):
    cb = ckv_ref[...].astype(BF16)
    kn = _dot(cb, wuk_ref[...])
    vt_ref[0] = _dot_nt(wuvt_ref[...], cb).astype(BF16)
    krb = kr_ref[...].astype(BF16)
    for j in range(N_PAIRS):
        c0 = 2 * LANES * j
        k_ref[:, c0:c0 + LANES] = kn[:, LANES * j:LANES * (j + 1)].astype(BF16)
        k_ref[:, c0 + LANES:c0 + 2 * LANES] = krb


def _kv_expand(ckv, kr128, wuk, wuv_t, n):
    tm = ATTN_TK
    row = lambda w: pl.BlockSpec((tm, w), lambda i: (i, 0))
    full = lambda a: pl.BlockSpec(a.shape, lambda i: (0, 0))
    nv = LANES * N_PAIRS
    return pl.pallas_call(
        _kv_expand_kernel,
        grid=(n // tm,),
        in_specs=[row(MLA_KV_LORA), row(LANES), full(wuk), full(wuv_t)],
        out_specs=[row(2 * LANES * N_PAIRS), pl.BlockSpec((1, nv, tm), lambda i: (i, 0, 0))],
        out_shape=[jax.ShapeDtypeStruct((n, 2 * LANES * N_PAIRS), BF16),
                   jax.ShapeDtypeStruct((n // tm, nv, tm), BF16)],
        compiler_params=_params("parallel"),
        name="mla_kv_expand",
    )(ckv, kr128, wuk, wuv_t)


def _pair_query_masks(width):
    lane = lax.broadcasted_iota(jnp.int32, (1, width), 1)
    half = LANES // 2
    r0 = LANES
    m0 = (lane < half) | ((lane >= r0) & (lane < r0 + MLA_ROPE))
    m1 = ((lane >= half) & (lane < LANES)) | ((lane >= r0 + MLA_ROPE) & (lane < r0 + 2 * MLA_ROPE))
    return m0, m1


def _mla_prompt_attn_kernel(q_ref, k_ref, vt_ref, o_ref, qh_sc, s_sc, bmax_sc, m_sc, acc_sc):
    i = pl.program_id(1)
    tq, tk = ATTN_TQ, ATTN_TK
    half = LANES // 2
    q = q_ref[...]
    for h, mask in enumerate(_pair_query_masks(2 * LANES)):
        qh_sc[h] = jnp.where(mask, q, jnp.zeros_like(q))
        m_sc[h] = jnp.full((1, tq), NEG_INF, F32)
        acc_sc[h] = jnp.zeros((half + ONES_ROWS, tq), F32)
    ones = jnp.ones((ONES_ROWS, tk), BF16)

    def scores(kb, h, slot):
        k = k_ref[pl.ds(pl.multiple_of(kb * tk, tk), tk), :]
        s = _dot_nt(k, qh_sc[h])
        s_sc[h, slot] = s
        bmax_sc[h, slot] = jnp.max(s, 0, keepdims=True)

    def update(kb, h, slot, visible=None):
        s = s_sc[h, slot]
        if visible is None:
            block_max = bmax_sc[h, slot]
        else:
            s = jnp.where(visible, s, NEG_INF)
            block_max = jnp.max(s, 0, keepdims=True)
        m_prev = m_sc[h]
        m_new = jnp.maximum(m_prev, block_max)
        alpha = jnp.exp2(m_prev - m_new)
        p = jnp.exp2(s - m_new).astype(BF16)
        vt = jnp.concatenate([vt_ref[kb, half * h:half * (h + 1), :], ones], axis=0)
        acc_sc[h] = alpha * acc_sc[h] + _dot(vt, p)
        m_sc[h] = m_new

    scores(0, 0, 0)
    scores(0, 1, 0)

    def advance(kb, slot):
        for h in range(2):
            scores(kb + 1, h, 1 - slot)
        for h in range(2):
            update(kb, h, slot)

    def body(kk, c):
        advance(2 * kk, 0)
        advance(2 * kk + 1, 1)
        return c

    lax.fori_loop(0, i // 2, body, 0)

    def diagonal(slot):
        k_chunk = lax.broadcasted_iota(jnp.int32, (tk, tq), 0) // CHUNK
        q_chunk = lax.broadcasted_iota(jnp.int32, (tk, tq), 1) // CHUNK
        for h in range(2):
            update(i, h, slot, k_chunk <= q_chunk)

    @pl.when(i % 2 == 0)
    def _():
        diagonal(0)

    @pl.when(i % 2 == 1)
    def _():
        advance(i - 1, 0)
        diagonal(1)

    o_t = jnp.concatenate([acc_sc[h, :half] / acc_sc[h, half:half + 1] for h in range(2)], axis=0)
    o_ref[...] = o_t.T.astype(o_ref.dtype)


def _mla_prompt_attn(q, k, vt):
    s = k.shape[0]
    tq = ATTN_TQ
    assert ATTN_TQ == ATTN_TK and vt.shape == (s // ATTN_TK, LANES * N_PAIRS, ATTN_TK)
    return pl.pallas_call(
        _mla_prompt_attn_kernel,
        grid=(N_PAIRS, s // tq),
        in_specs=[pl.BlockSpec((tq, 2 * LANES), lambda j, i: (i, j)),
                  pl.BlockSpec((s, 2 * LANES), lambda j, i: (0, j)),
                  pl.BlockSpec((s // ATTN_TK, LANES, ATTN_TK), lambda j, i: (0, j, 0))],
        out_specs=pl.BlockSpec((tq, LANES), lambda j, i: (i, j)),
        out_shape=jax.ShapeDtypeStruct((s, LANES * N_PAIRS), BF16),
        scratch_shapes=[pltpu.VMEM((2, tq, 2 * LANES), BF16),
                        pltpu.VMEM((2, 2, ATTN_TK, tq), F32),
                        pltpu.VMEM((2, 2, 1, tq), F32),
                        pltpu.VMEM((2, 1, tq), F32),
                        pltpu.VMEM((2, LANES // 2 + ONES_ROWS, tq), F32)],
        compiler_params=_params("parallel", "arbitrary"),
        name="mla_prompt_attn",
    )(q, k, vt)


def _static_visible(q_pos, k_pos, left_chunks=None):
    qc = (q_pos // CHUNK)[:, None]
    kc = (k_pos // CHUNK)[None, :]
    vis = kc <= qc
    if left_chunks is not None:
        vis = vis & (kc >= qc - left_chunks) & (k_pos[None, :] >= 0)
    return vis


def _mla_sample_attn_kernel(q_ref, cc_ref, rc_ref, cn_ref, rn_ref, wukt_ref, wuv_ref, o_ref, *,
                            visible):
    nope_masks = _pair_query_masks(2 * LANES)
    lane = lax.broadcasted_iota(jnp.int32, (1, LANES), 1)
    low = lane < LANES // 2
    q_lat, q_rope = [], []
    for j in range(N_PAIRS):
        q = q_ref[0, :, 2 * LANES * j:2 * LANES * (j + 1)]
        for h in range(2):
            qm = jnp.where(nope_masks[h], q, jnp.zeros_like(q))
            q_lat.append(_dot(qm[:, :LANES], wukt_ref[LANES * j:LANES * (j + 1), :]).astype(BF16))
            q_rope.append(qm[:, LANES:])
    q_lat = jnp.concatenate(q_lat, axis=0)
    q_rope = jnp.concatenate(q_rope, axis=0)
    cc = cc_ref[0].astype(BF16)
    cn = cn_ref[0].astype(BF16)
    s_c = _dot_nt(q_lat, cc) + _dot_nt(q_rope, rc_ref[0].astype(BF16))
    s_n = _dot_nt(q_lat, cn) + _dot_nt(q_rope, rn_ref[0].astype(BF16))
    if visible is not None:
        vis_c, vis_n = (jnp.asarray(np.tile(v, (MLA_HEADS, 1))) for v in visible)
        s_c = jnp.where(vis_c, s_c, NEG_INF)
        s_n = jnp.where(vis_n, s_n, NEG_INF)
    m = jnp.maximum(jnp.max(s_c, -1, keepdims=True), jnp.max(s_n, -1, keepdims=True))
    p_c = jnp.exp2(s_c - m)
    p_n = jnp.exp2(s_n - m)
    l = jnp.sum(p_c, -1, keepdims=True) + jnp.sum(p_n, -1, keepdims=True)
    o_lat = ((_dot(p_c.astype(BF16), cc) + _dot(p_n.astype(BF16), cn)) / l).astype(BF16)
    t = q_ref.shape[1]
    for j in range(N_PAIRS):
        w = wuv_ref[:, LANES * j:LANES * (j + 1)]
        o0 = _dot(o_lat[2 * j * t:(2 * j + 1) * t], w)
        o1 = _dot(o_lat[(2 * j + 1) * t:(2 * j + 2) * t], w)
        o_ref[0, :, LANES * j:LANES * (j + 1)] = jnp.where(low, o0, o1).astype(o_ref.dtype)


def _mla_sample_attn(q, cache_ckv, cache_kr128, new_ckv, new_kr128, wuk_t, wuv):
    b, t, _ = q.shape
    n_past = cache_ckv.shape[1]
    q_pos = PAST_LEN + np.arange(t)
    vis_c = _static_visible(q_pos, PAST_LEN - n_past + np.arange(n_past))
    vis_n = _static_visible(q_pos, q_pos)
    visible = None if (vis_c.all() and vis_n.all()) else (vis_c, vis_n)
    blk = lambda a: pl.BlockSpec((1,) + a.shape[1:], lambda i: (i, 0, 0))
    full = lambda a: pl.BlockSpec(a.shape, lambda i: (0, 0))
    return pl.pallas_call(
        functools.partial(_mla_sample_attn_kernel, visible=visible),
        grid=(b,),
        in_specs=[blk(q), blk(cache_ckv), blk(cache_kr128), blk(new_ckv), blk(new_kr128),
                  full(wuk_t), full(wuv)],
        out_specs=pl.BlockSpec((1, t, LANES * N_PAIRS), lambda i: (i, 0, 0)),
        out_shape=jax.ShapeDtypeStruct((b, t, LANES * N_PAIRS), BF16),
        compiler_params=_params("parallel"),
        name="mla_sample_attn",
    )(q, cache_ckv, cache_kr128, new_ckv, new_kr128, wuk_t, wuv)


def _out_ln_kernel(op_ref, os_ref, w_ref, x_ref, g_ref, b_ref, y_ref, *, prompt_tiles):
    o = jnp.where(pl.program_id(0) < prompt_tiles, op_ref[...], os_ref[...])
    y = DEEPNORM_ALPHA * x_ref[...] + _dot(o, w_ref[...])
    y_ref[...] = _layer_norm(y, g_ref[...], b_ref[...])


def _out_ln(o_p, o_s, w_o, x, g, b):
    n = x.shape[0]
    tm = TOKEN_TILE
    n_p = o_p.shape[0] // tm
    assert o_p.shape[0] + o_s.shape[0] == n
    row = pl.BlockSpec((tm, D_MODEL), lambda i: (i, 0))
    row_p = pl.BlockSpec((tm, D_MODEL), lambda i: (jnp.minimum(i, n_p - 1), 0))
    row_s = pl.BlockSpec((tm, D_MODEL), lambda i: (jnp.maximum(i - n_p, 0), 0))
    full = lambda a: pl.BlockSpec(a.shape, lambda i: (0, 0))
    return pl.pallas_call(
        functools.partial(_out_ln_kernel, prompt_tiles=n_p),
        grid=(n // tm,),
        in_specs=[row_p, row_s, full(w_o), row, full(g), full(b)],
        out_specs=row,
        out_shape=jax.ShapeDtypeStruct((n, D_MODEL), F32),
        compiler_params=_params("parallel"),
        name="out_proj_ln",
    )(o_p, o_s, w_o, x, g, b)


def _band_qkv_kernel(x_ref, wqk_ref, wv_ref, wvt_ref, q_ref, k_ref, vt_ref, kf_ref, vf_ref, *,
                     first_keep_tile):
    xb = x_ref[...].astype(BF16)
    qk = _dot(xb, wqk_ref[...])
    scale = math.log2(math.e) / math.sqrt(BAND_HEAD_DIM)
    q_ref[...] = (qk[:, :D_MODEL] * scale).astype(BF16)
    k = qk[:, D_MODEL:]
    k_ref[...] = k.astype(BF16)
    vt = _dot_nt(wvt_ref[...], xb).astype(BF16)
    for t in range(vt_ref.shape[0]):
        vt_ref[t] = vt[:, LANES * t:LANES * (t + 1)]

    @pl.when(pl.program_id(0) >= first_keep_tile)
    def _():
        kf_ref[...] = k
        vf_ref[...] = _dot(xb, wv_ref[...])


def _band_qkv(x, w_qkv, n_keep):
    n = x.shape[0]
    tm = TOKEN_TILE
    first = (n - n_keep) // tm
    w = w_qkv.astype(BF16)
    wqk, wv = w[:, :2 * D_MODEL], w[:, 2 * D_MODEL:]
    row = pl.BlockSpec((tm, D_MODEL), lambda i: (i, 0))
    keep = pl.BlockSpec((tm, D_MODEL), lambda i: (jnp.maximum(i - first, 0), 0))
    full = lambda a: pl.BlockSpec(a.shape, lambda i: (0, 0))
    bf = jax.ShapeDtypeStruct((n, D_MODEL), BF16)
    kf = jax.ShapeDtypeStruct((n_keep, D_MODEL), F32)
    return pl.pallas_call(
        functools.partial(_band_qkv_kernel, first_keep_tile=first),
        grid=(n // tm,),
        in_specs=[row, full(wqk), full(wv), full(wv)],
        out_specs=[row, row, pl.BlockSpec((tm // LANES, D_MODEL, LANES), lambda i: (i, 0, 0)),
                   keep, keep],
        out_shape=[bf, bf, jax.ShapeDtypeStruct((n // LANES, D_MODEL, LANES), BF16), kf, kf],
        compiler_params=_params("arbitrary"),
        name="band_qkv",
    )(x, wqk, wv, wv.T)


def _band_prompt_attn_kernel(q_ref, k_ref, vt_ref, bias_ref, o_ref):
    i = pl.program_id(1)
    half = LANES // 2
    low = lax.broadcasted_iota(jnp.int32, (1, LANES), 1) < half
    n_win = BAND_WIN // LANES
    n_left = BAND_KEEP_MAX // LANES

    def body(g, c):
        work = []
        for u in range(BAND_GROUP):
            sb = g * BAND_GROUP + u
            blk = i * (BAND_QBLOCK // BAND_QSUB) + sb
            b0 = jnp.maximum(blk - n_left, 0)
            rows = pl.ds(pl.multiple_of(sb * BAND_QSUB, BAND_QSUB), BAND_QSUB)
            q = q_ref[rows, :]
            zero = jnp.zeros_like(q)
            qq = jnp.concatenate([jnp.where(low, q, zero), jnp.where(low, zero, q)], axis=0)
            kw = k_ref[pl.ds(pl.multiple_of(b0 * LANES, LANES), BAND_WIN), :]
            s = _dot_nt(kw, qq) + bias_ref[0, jnp.minimum(blk, n_left)]
            work.append((rows, b0, s))
        for rows, b0, s in work:
            m = jnp.max(s, 0, keepdims=True)
            p = jnp.exp2(s - m)
            l = jnp.sum(p, 0, keepdims=True)
            pb = p.astype(BF16)
            vt = jnp.concatenate([vt_ref[b0 + w] for w in range(n_win)], axis=1)
            o_t = jnp.concatenate(
                [_dot(vt[half * h:half * (h + 1)], pb[:, LANES * h:LANES * (h + 1)])
                 / l[:, LANES * h:LANES * (h + 1)] for h in range(2)], axis=0)
            o_ref[rows, :] = o_t.T.astype(o_ref.dtype)
        return c

    lax.fori_loop(0, BAND_QBLOCK // BAND_QSUB // BAND_GROUP, body, 0)


def _band_prompt_attn(q, k, vt, bias, s):
    return pl.pallas_call(
        _band_prompt_attn_kernel,
        grid=(N_PAIRS, s // BAND_QBLOCK),
        in_specs=[pl.BlockSpec((BAND_QBLOCK, LANES), lambda j, i: (i, j)),
                  pl.BlockSpec((s, LANES), lambda j, i: (0, j)),
                  pl.BlockSpec((s // LANES, LANES, LANES), lambda j, i: (0, j, 0)),
                  pl.BlockSpec((1,) + bias.shape[1:], lambda j, i: (j, 0, 0, 0))],
        out_specs=pl.BlockSpec((BAND_QBLOCK, LANES), lambda j, i: (i, j)),
        out_shape=jax.ShapeDtypeStruct((s, D_MODEL), BF16),
        compiler_params=_params("parallel", "arbitrary"),
        name="band_prompt_attn",
    )(q, k, vt, bias)


def _band_sample_attn_kernel(q_ref, kc_ref, vc_ref, kn_ref, vn_ref, bc_ref, bn_ref, o_ref):
    lane = lax.broadcasted_iota(jnp.int32, (1, LANES), 1)
    low = lane < LANES // 2
    for j in range(N_PAIRS):
        cols = slice(LANES * j, LANES * (j + 1))
        q = q_ref[0, :, cols]
        kc = kc_ref[0, :, cols].astype(BF16)
        vc = vc_ref[0, :, cols].astype(BF16)
        kn = kn_ref[0, :, cols].astype(BF16)
        vn = vn_ref[0, :, cols].astype(BF16)
        outs = []
        for h in range(2):
            qh = jnp.where(low if h == 0 else ~low, q, jnp.zeros_like(q))
            sc = _dot_nt(qh, kc) + bc_ref[2 * j + h]
            sn = _dot_nt(qh, kn) + bn_ref[2 * j + h]
            m = jnp.maximum(jnp.max(sc, -1, keepdims=True), jnp.max(sn, -1, keepdims=True))
            pc = jnp.exp2(sc - m)
            pn = jnp.exp2(sn - m)
            l = jnp.sum(pc, -1, keepdims=True) + jnp.sum(pn, -1, keepdims=True)
            outs.append((_dot(pc.astype(BF16), vc) + _dot(pn.astype(BF16), vn)) / l)
        o_ref[0, :, cols] = jnp.where(low, outs[0], outs[1]).astype(o_ref.dtype)


def _band_sample_attn(q, kc, vc, kn, vn, bias_c, bias_n):
    b, t, _ = q.shape
    blk = lambda a: pl.BlockSpec((1,) + a.shape[1:], lambda i: (i, 0, 0))
    full = lambda a: pl.BlockSpec(a.shape, lambda i: (0, 0, 0))
    return pl.pallas_call(
        _band_sample_attn_kernel,
        grid=(b,),
        in_specs=[blk(q), blk(kc), blk(vc), blk(kn), blk(vn), full(bias_c), full(bias_n)],
        out_specs=pl.BlockSpec((1, t, D_MODEL), lambda i: (i, 0, 0)),
        out_shape=jax.ShapeDtypeStruct((b, t, D_MODEL), BF16),
        compiler_params=_params("parallel"),
        name="band_sample_attn",
    )(q, kc, vc, kn, vn, bias_c, bias_n)


def _band_bias(rel_table, q_pos, k_pos):
    nq, nk = len(q_pos), len(k_pos)
    assert (np.diff(q_pos) == 1).all() and (np.diff(k_pos) == 1).all()
    span = nq + nk - 1
    diag = int(q_pos[0] - k_pos[0]) + nq - 1 - np.arange(span)
    u = rel_table.astype(F32)[:, np.clip(diag, -REL_MAX, REL_MAX) + REL_MAX] * math.log2(math.e)
    u = jnp.pad(u, ((0, 0), (0, 1)))
    skew = jnp.tile(u, (1, nq))[:, :nq * span].reshape(-1, nq, span)
    bias = skew[:, :, nq - 1:nq - 1 + nk]
    vis = _static_visible(q_pos, k_pos, LEFT_CHUNKS)
    return jnp.where(jnp.asarray(vis)[None], bias, NEG_INF)


def _route(logits_t, rb_ref):
    scores = jax.nn.sigmoid(logits_t)
    rows = [scores[e:e + 1, :] for e in range(N_EXPERTS)]
    biased = [rows[e] + rb_ref[e] for e in range(N_EXPERTS)]
    best = None
    sel = None
    for g in range(N_GROUPS):
        a, b, c, d = biased[EXPERTS_PER_GROUP * g:EXPERTS_PER_GROUP * (g + 1)]
        top2 = jnp.maximum(jnp.maximum(jnp.maximum(a + b, a + c), jnp.maximum(a + d, b + c)),
                           jnp.maximum(b + d, c + d))
        if g == 0:
            best, sel = top2, jnp.zeros(top2.shape, jnp.int32)
        else:
            better = top2 > best
            sel = jnp.where(better, g, sel)
            best = jnp.where(better, top2, best)
    cand = [jnp.where(sel == e // EXPERTS_PER_GROUP, biased[e], NEG_INF) for e in range(N_EXPERTS)]

    def argmax_first(vals):
        top = functools.reduce(jnp.maximum, vals)
        idx = jnp.full(top.shape, N_EXPERTS, jnp.int32)
        for e in reversed(range(N_EXPERTS)):
            idx = jnp.where(vals[e] == top, e, idx)
        return idx

    i1 = argmax_first(cand)
    i2 = argmax_first([jnp.where(i1 == e, -jnp.inf, cand[e]) for e in range(N_EXPERTS)])
    picked = [jnp.where((i1 == e) | (i2 == e), rows[e], 0.0) for e in range(N_EXPERTS)]
    total = functools.reduce(jnp.add, picked)
    return [p / total for p in picked]


def _moe_kernel(x_ref, rw_ref, rb_ref, wg_ref, wu_ref, wd_ref, g_ref, b_ref, y_ref,
                xb_sc, comb_sc, acc_sc):
    grp = pl.program_id(1)

    @pl.when(grp == 0)
    def _():
        x = x_ref[...]
        xb = x.astype(BF16)
        xb_sc[...] = xb
        acc_sc[...] = DEEPNORM_ALPHA * x
        x_lo = (x - xb.astype(F32)).astype(BF16)
        rw_hi = rw_ref[0]
        logits_t = _dot_nt(rw_hi, xb) + (_dot_nt(rw_hi, x_lo) + _dot_nt(rw_ref[1], xb))
        comb_rows = _route(logits_t, rb_ref)
        comb = jnp.concatenate(comb_rows, axis=0).T
        for g in range(N_GROUPS):
            comb_sc[g] = comb[:, EXPERTS_PER_GROUP * g:EXPERTS_PER_GROUP * (g + 1)]

    xb = xb_sc[...]
    comb = comb_sc[grp]
    parts = []
    for e in range(EXPERTS_PER_GROUP):
        h = jax.nn.silu(_dot(xb, wg_ref[e])) * _dot(xb, wu_ref[e])
        parts.append((h * comb[:, e:e + 1]).astype(BF16))
    acc_sc[...] += _dot(jnp.concatenate(parts, axis=1), wd_ref[0])

    @pl.when(grp == N_GROUPS - 1)
    def _():
        y_ref[...] = _layer_norm(acc_sc[...], g_ref[...], b_ref[...])


def _moe_ln(x, rw_t, rb, wg, wu, wd, g, b):
    n = x.shape[0]
    tm = MOE_TILE
    row = pl.BlockSpec((tm, D_MODEL), lambda i, j: (i, 0))
    full = lambda a: pl.BlockSpec(a.shape, lambda i, j: (0, 0))
    grp = lambda a: pl.BlockSpec((a.shape[0] // N_GROUPS,) + a.shape[1:], lambda i, j: (j, 0, 0))
    return pl.pallas_call(
        _moe_kernel,
        grid=(n // tm, N_GROUPS),
        in_specs=[row, pl.BlockSpec(rw_t.shape, lambda i, j: (0, 0, 0)),
                  pl.BlockSpec(memory_space=pltpu.SMEM),
                  grp(wg), grp(wu), grp(wd), full(g), full(b)],
        out_specs=row,
        out_shape=jax.ShapeDtypeStruct((n, D_MODEL), F32),
        scratch_shapes=[pltpu.VMEM((tm, D_MODEL), BF16),
                        pltpu.VMEM((N_GROUPS, tm, EXPERTS_PER_GROUP), F32),
                        pltpu.VMEM((tm, D_MODEL), F32)],
        compiler_params=_params("parallel", "arbitrary"),
        name="moe_ln",
    )(x, rw_t, rb, wg, wu, wd, g, b)


def _rope_tables():
    half = MLA_ROPE // 2
    inv = ROPE_THETA ** (-jnp.arange(half, dtype=F32) / half)
    pos = jnp.concatenate([jnp.arange(SEQ, dtype=jnp.int32),
                           jnp.tile(PAST_LEN + jnp.arange(DEC_SEQ, dtype=jnp.int32), DEC_BATCH)])
    ang = pos.astype(F32)[:, None] * inv[None, :]
    reps = LANES // half
    return jnp.tile(jnp.cos(ang), (1, reps)), jnp.tile(jnp.sin(ang), (1, reps))


def _rot_cols(w):
    half = MLA_ROPE // 2
    return jnp.concatenate([-w[..., half:], w[..., :half]], -1)


def _mla_weights(w_in, w_uq, w_uk, w_uv, w_o):
    o_kv = MLA_Q_LORA + MLA_KV_LORA
    w_r = w_in[:, o_kv:]
    zpad = jnp.zeros((D_MODEL, LANES - 2 * MLA_ROPE), F32)
    w_in_p = jnp.concatenate([w_in[:, :o_kv], w_r, w_r, zpad,
                              _rot_cols(w_r), _rot_cols(w_r), zpad], 1).astype(BF16)
    wq = w_uq.reshape(MLA_Q_LORA, N_PAIRS, 2, MLA_QK)
    nope = wq[..., :MLA_NOPE].reshape(MLA_Q_LORA, N_PAIRS, 2 * MLA_NOPE)
    ropew = wq[..., MLA_NOPE:]
    zq = jnp.zeros((MLA_Q_LORA, N_PAIRS, LANES - 2 * MLA_ROPE), F32)
    rope_cols = jnp.concatenate([ropew.reshape(MLA_Q_LORA, N_PAIRS, 2 * MLA_ROPE), zq], -1)
    rot_cols = jnp.concatenate([_rot_cols(ropew).reshape(MLA_Q_LORA, N_PAIRS, 2 * MLA_ROPE), zq], -1)
    wqa = jnp.concatenate([nope, rope_cols], -1).reshape(MLA_Q_LORA, 2 * LANES * N_PAIRS).astype(BF16)
    wqb = rot_cols.reshape(MLA_Q_LORA, LANES * N_PAIRS).astype(BF16)
    return w_in_p, wqa, wqb, w_uk.astype(BF16), w_uv.astype(BF16), w_o.astype(BF16)


def _group_experts(w_gate, w_up, w_down):
    wd = w_down.reshape(N_GROUPS, EXPERTS_PER_GROUP * D_EXPERT, D_MODEL).astype(BF16)
    return w_gate.astype(BF16), w_up.astype(BF16), wd


def kernel(x_prompt, x_sample, cache_mla_ckv, cache_mla_krope, cache_band_k, cache_band_v,
           mla_w_in, mla_q_norm, mla_kv_norm, mla_w_uq, mla_w_uk, mla_w_uv, mla_w_o,
           band_w_qkv, band_rel_bias, band_w_o,
           router_w, router_b, moe_w_gate, moe_w_up, moe_w_down,
           ln_mix_g, ln_mix_b, ln_ffn_g, ln_ffn_b):
    x = jnp.concatenate([x_prompt.reshape(SEQ, D_MODEL), x_sample.reshape(N_SAMPLE, D_MODEL)], 0)
    cos, sin = _rope_tables()
    rw_f = router_w.T.astype(F32)
    rw_hi = rw_f.astype(BF16)
    rw_t = jnp.stack([rw_hi, (rw_f - rw_hi.astype(F32)).astype(BF16)], 0)
    rb = router_b.astype(F32)
    band_keep = cache_band_k.shape[2]
    row2 = lambda a: a.reshape(1, -1)

    ckv_p, kr_p, ckv_s, kr_s = [], [], [], []
    bk_p, bv_p, bk_s, bv_s = [], [], [], []
    for i in range(DEPTH):
        j = i // N_MIXERS
        if i % N_MIXERS == 0:
            w_in_p, wqa, wqb, wuk, wuv, w_o = _mla_weights(
                mla_w_in[j], mla_w_uq[j], mla_w_uk[j], mla_w_uv[j], mla_w_o[j])
            ckv, kr128, q = _mla_proj(x, w_in_p, row2(mla_q_norm[j]), row2(mla_kv_norm[j]),
                                      wqa, wqb, cos, sin)
            kr = kr128[:, :MLA_ROPE]
            ckv_p.append(ckv[:SEQ].reshape(1, SEQ, MLA_KV_LORA))
            kr_p.append(kr[:SEQ].reshape(1, SEQ, MLA_ROPE))
            ckv_s.append(ckv[SEQ:].reshape(DEC_BATCH, DEC_SEQ, MLA_KV_LORA))
            kr_s.append(kr[SEQ:].reshape(DEC_BATCH, DEC_SEQ, MLA_ROPE))
            k_p, vt_p = _kv_expand(ckv, kr128, wuk, wuv.T, SEQ)
            o_p = _mla_prompt_attn(q, k_p, vt_p)
            cache_kr = cache_mla_krope[j]
            cache_kr128 = jnp.concatenate(
                [cache_kr, cache_kr, jnp.zeros(cache_kr.shape[:2] + (LANES - 2 * MLA_ROPE,), F32)], -1)
            o_s = _mla_sample_attn(q[SEQ:].reshape(DEC_BATCH, DEC_SEQ, -1), cache_mla_ckv[j], cache_kr128,
                                   ckv_s[-1], kr128[SEQ:].reshape(DEC_BATCH, DEC_SEQ, LANES), wuk.T, wuv)
        else:
            n_keep = min(BAND_KEEP_MAX, SEQ) + N_SAMPLE
            q, k, vt, k_f, v_f = _band_qkv(x, band_w_qkv[j], n_keep)
            keep = n_keep - N_SAMPLE
            hd = (BAND_HEADS, BAND_HEAD_DIM)
            bk_p.append(k_f[:keep].reshape((1, keep) + hd))
            bv_p.append(v_f[:keep].reshape((1, keep) + hd))
            bk_s.append(k_f[keep:].reshape((DEC_BATCH, DEC_SEQ) + hd))
            bv_s.append(v_f[keep:].reshape((DEC_BATCH, DEC_SEQ) + hd))
            w_o = band_w_o[j].astype(BF16)
            q_off = np.arange(BAND_QSUB)
            starts = [BAND_QSUB * t for t in range(BAND_KEEP_MAX // BAND_QSUB)] + [BAND_KEEP_MAX]
            tiles = jnp.stack([_band_bias(band_rel_bias[j], st + q_off, np.arange(BAND_WIN))
                               for st in starts], 1)
            bias_p = (tiles.reshape(N_PAIRS, 2, len(starts), BAND_QSUB, BAND_WIN)
                      .transpose(0, 2, 4, 1, 3).reshape(N_PAIRS, len(starts), BAND_WIN, 2 * BAND_QSUB))
            o_p = _band_prompt_attn(q, k, vt, bias_p, SEQ)
            q_pos = PAST_LEN + np.arange(DEC_SEQ)
            pos_c = PAST_LEN - band_keep + np.arange(band_keep)
            bias_c = _band_bias(band_rel_bias[j], q_pos, pos_c)
            bias_n = _band_bias(band_rel_bias[j], q_pos, q_pos)
            s3 = lambda a: a.reshape(DEC_BATCH, DEC_SEQ, D_MODEL)
            o_s = _band_sample_attn(s3(q[SEQ:]), cache_band_k[j].reshape(DEC_BATCH, band_keep, D_MODEL),
                                    cache_band_v[j].reshape(DEC_BATCH, band_keep, D_MODEL),
                                    s3(k_f[keep:]), s3(v_f[keep:]), bias_c, bias_n)
        x = _out_ln(o_p, o_s.reshape(N_SAMPLE, D_MODEL), w_o, x, row2(ln_mix_g[i]), row2(ln_mix_b[i]))
        wg, wu, wd = _group_experts(moe_w_gate[i], moe_w_up[i], moe_w_down[i])
        x = _moe_ln(x, rw_t, rb, wg, wu, wd, row2(ln_ffn_g[i]), row2(ln_ffn_b[i]))

    return (x[:SEQ].reshape(1, SEQ, D_MODEL), x[SEQ:].reshape(DEC_BATCH, DEC_SEQ, D_MODEL),
            jnp.stack(ckv_p, 0), jnp.stack(kr_p, 0), jnp.stack(ckv_s, 0), jnp.stack(kr_s, 0),
            jnp.stack(bk_p, 0), jnp.stack(bv_p, 0), jnp.stack(bk_s, 0), jnp.stack(bv_s, 0))
```

```python
import functools
import math

import numpy as np
import jax
import jax.numpy as jnp
from jax import lax
from jax.experimental import pallas as pl
from jax.experimental.pallas import tpu as pltpu

D_MODEL = 1024
SEQ = 16384
DEPTH = 2
DEC_BATCH = 32
DEC_SEQ = 32
PAST_LEN = 1024
CHUNK = 64
N_MIXERS = 2

MLA_HEADS = 16
MLA_Q_LORA = 384
MLA_KV_LORA = 256
MLA_NOPE = 64
MLA_ROPE = 32
MLA_QK = MLA_NOPE + MLA_ROPE
MLA_V = 64
ROPE_THETA = 10000.0

BAND_HEADS = 16
BAND_HEAD_DIM = D_MODEL // BAND_HEADS
LEFT_CHUNKS = 8
BAND_KEEP_MAX = LEFT_CHUNKS * CHUNK
REL_MAX = 128

N_EXPERTS = 16
N_GROUPS = 4
EXPERTS_PER_GROUP = N_EXPERTS // N_GROUPS
D_EXPERT = 256

DEEPNORM_ALPHA = (2.0 * DEPTH) ** 0.25
NORM_EPS = 1e-5
NEG_INF = -1e30

N_SAMPLE = DEC_BATCH * DEC_SEQ
N_TOKENS = SEQ + N_SAMPLE
N_PAIRS = MLA_HEADS // 2
LANES = 128
VMEM_LIMIT = 56 * 1024 * 1024

TOKEN_TILE = 512
MOE_TILE = 1024
ATTN_TQ = 512
ATTN_TK = 512
ONES_ROWS = 16
BAND_QSUB = 128
BAND_WIN = BAND_KEEP_MAX + BAND_QSUB
BAND_QBLOCK = 1024
BAND_GROUP = 4

BF16 = jnp.bfloat16
F32 = jnp.float32


def _params(*sem):
    return pltpu.CompilerParams(dimension_semantics=sem, vmem_limit_bytes=VMEM_LIMIT)


def _dot(a, b):
    return jnp.dot(a, b, preferred_element_type=F32)


def _dot_nt(a, b):
    return lax.dot_general(a, b, (((1,), (1,)), ((), ())), preferred_element_type=F32)


def _layer_norm(y, g, b):
    mu = jnp.mean(y, -1, keepdims=True)
    d = y - mu
    var = jnp.mean(d * d, -1, keepdims=True)
    return d * lax.rsqrt(var + NORM_EPS) * g + b


def _rms_norm(y, g):
    return y * lax.rsqrt(jnp.mean(y * y, -1, keepdims=True) + NORM_EPS) * g


def _stream_specs(tm, width, prompt_tiles):
    return [pl.BlockSpec((tm, width), lambda i, *_: (jnp.minimum(i, prompt_tiles - 1), 0)),
            pl.BlockSpec((tm, width), lambda i, *_: (jnp.maximum(i - prompt_tiles, 0), 0))]


def _stream_tile(p_ref, s_ref, prompt_tiles):
    return jnp.where(pl.program_id(0) < prompt_tiles, p_ref[...], s_ref[...])


def _mla_proj_kernel(xp_ref, xs_ref, w_in_ref, qn_ref, kvn_ref, wqa_ref, wqb_ref, cos_ref, sin_ref,
                     ckv_ref, kr_ref, q_ref, *, prompt_tiles):
    xb = _stream_tile(xp_ref, xs_ref, prompt_tiles).astype(BF16)
    a = _dot(xb, w_in_ref[...])
    cq = _rms_norm(a[:, :MLA_Q_LORA], qn_ref[...])
    o_kv = MLA_Q_LORA + MLA_KV_LORA
    ckv_ref[...] = _rms_norm(a[:, MLA_Q_LORA:o_kv], kvn_ref[...])
    cos = cos_ref[...]
    sin = sin_ref[...]
    kr_ref[...] = a[:, o_kv:o_kv + LANES] * cos + a[:, o_kv + LANES:] * sin
    cqb = cq.astype(BF16)
    qa = _dot(cqb, wqa_ref[...])
    qb = _dot(cqb, wqb_ref[...])
    scale = math.log2(math.e) / math.sqrt(MLA_QK)
    for j in range(N_PAIRS):
        c0 = 2 * LANES * j
        q_ref[:, c0:c0 + LANES] = (qa[:, c0:c0 + LANES] * scale).astype(BF16)
        rot = qa[:, c0 + LANES:c0 + 2 * LANES] * cos + qb[:, LANES * j:LANES * (j + 1)] * sin
        q_ref[:, c0 + LANES:c0 + 2 * LANES] = (rot * scale).astype(BF16)


def _mla_proj(x_p, x_s, w_in_p, qn, kvn, wqa, wqb, cos, sin):
    n = x_p.shape[0] + x_s.shape[0]
    tm = TOKEN_TILE
    n_p = x_p.shape[0] // tm
    row = lambda w: pl.BlockSpec((tm, w), lambda i: (i, 0))
    full = lambda a: pl.BlockSpec(a.shape, lambda i: (0, 0))
    return pl.pallas_call(
        functools.partial(_mla_proj_kernel, prompt_tiles=n_p),
        grid=(n // tm,),
        in_specs=_stream_specs(tm, D_MODEL, n_p) + [full(w_in_p), full(qn), full(kvn), full(wqa),
                                                    full(wqb), row(LANES), row(LANES)],
        out_specs=[row(MLA_KV_LORA), row(LANES), row(2 * LANES * N_PAIRS)],
        out_shape=[jax.ShapeDtypeStruct((n, MLA_KV_LORA), F32),
                   jax.ShapeDtypeStruct((n, LANES), F32),
                   jax.ShapeDtypeStruct((n, 2 * LANES * N_PAIRS), BF16)],
        compiler_params=_params("parallel"),
        name="mla_proj",
    )(x_p, x_s, w_in_p, qn, kvn, wqa, wqb, cos, sin)


def _kv_expand_kernel(ckv_ref, kr_ref, wuk_ref, wuvt_ref, k_ref, vt_ref):
    cb = ckv_ref[...].astype(BF16)
    kn = _dot(cb, wuk_ref[...])
    vt_ref[0] = _dot_nt(wuvt_ref[...], cb).astype(BF16)
    krb = kr_ref[...].astype(BF16)
    for j in range(N_PAIRS):
        c0 = 2 * LANES * j
        k_ref[:, c0:c0 + LANES] = kn[:, LANES * j:LANES * (j + 1)].astype(BF16)
        k_ref[:, c0 + LANES:c0 + 2 * LANES] = krb


def _kv_expand(ckv, kr128, wuk, wuv_t, n):
    tm = ATTN_TK
    row = lambda w: pl.BlockSpec((tm, w), lambda i: (i, 0))
    full = lambda a: pl.BlockSpec(a.shape, lambda i: (0, 0))
    nv = LANES * N_PAIRS
    return pl.pallas_call(
        _kv_expand_kernel,
        grid=(n // tm,),
        in_specs=[row(MLA_KV_LORA), row(LANES), full(wuk), full(wuv_t)],
        out_specs=[row(2 * LANES * N_PAIRS), pl.BlockSpec((1, nv, tm), lambda i: (i, 0, 0))],
        out_shape=[jax.ShapeDtypeStruct((n, 2 * LANES * N_PAIRS), BF16),
                   jax.ShapeDtypeStruct((n // tm, nv, tm), BF16)],
        compiler_params=_params("parallel"),
        name="mla_kv_expand",
    )(ckv, kr128, wuk, wuv_t)


def _pair_query_masks(width):
    lane = lax.broadcasted_iota(jnp.int32, (1, width), 1)
    half = LANES // 2
    r0 = LANES
    m0 = (lane < half) | ((lane >= r0) & (lane < r0 + MLA_ROPE))
    m1 = ((lane >= half) & (lane < LANES)) | ((lane >= r0 + MLA_ROPE) & (lane < r0 + 2 * MLA_ROPE))
    return m0, m1


def _mla_prompt_attn_kernel(q_ref, k_ref, vt_ref, o_ref, qh_sc, s_sc, bmax_sc, m_sc, acc_sc):
    i = pl.program_id(1)
    tq, tk = ATTN_TQ, ATTN_TK
    half = LANES // 2
    q = q_ref[...]
    for h, mask in enumerate(_pair_query_masks(2 * LANES)):
        qh_sc[h] = jnp.where(mask, q, jnp.zeros_like(q))
        m_sc[h] = jnp.full((1, tq), NEG_INF, F32)
        acc_sc[h] = jnp.zeros((half + ONES_ROWS, tq), F32)
    ones = jnp.ones((ONES_ROWS, tk), BF16)

    def scores(kb, h, slot):
        k = k_ref[pl.ds(pl.multiple_of(kb * tk, tk), tk), :]
        s = _dot_nt(k, qh_sc[h])
        s_sc[h, slot] = s
        bmax_sc[h, slot] = jnp.max(s, 0, keepdims=True)

    def update(kb, h, slot, visible=None):
        s = s_sc[h, slot]
        if visible is None:
            block_max = bmax_sc[h, slot]
        else:
            s = jnp.where(visible, s, NEG_INF)
            block_max = jnp.max(s, 0, keepdims=True)
        m_prev = m_sc[h]
        m_new = jnp.maximum(m_prev, block_max)
        alpha = jnp.exp2(m_prev - m_new)
        p = jnp.exp2(s - m_new).astype(BF16)
        vt = jnp.concatenate([vt_ref[kb, half * h:half * (h + 1), :], ones], axis=0)
        acc_sc[h] = alpha * acc_sc[h] + _dot(vt, p)
        m_sc[h] = m_new

    scores(0, 0, 0)
    scores(0, 1, 0)

    def advance(kb, slot):
        for h in range(2):
            scores(kb + 1, h, 1 - slot)
        for h in range(2):
            update(kb, h, slot)

    def body(kk, c):
        advance(2 * kk, 0)
        advance(2 * kk + 1, 1)
        return c

    lax.fori_loop(0, i // 2, body, 0)

    def diagonal(slot):
        k_chunk = lax.broadcasted_iota(jnp.int32, (tk, tq), 0) // CHUNK
        q_chunk = lax.broadcasted_iota(jnp.int32, (tk, tq), 1) // CHUNK
        for h in range(2):
            update(i, h, slot, k_chunk <= q_chunk)

    @pl.when(i % 2 == 0)
    def _():
        diagonal(0)

    @pl.when(i % 2 == 1)
    def _():
        advance(i - 1, 0)
        diagonal(1)

    o_t = jnp.concatenate([acc_sc[h, :half] / acc_sc[h, half:half + 1] for h in range(2)], axis=0)
    o_ref[...] = o_t.T.astype(o_ref.dtype)


def _mla_prompt_attn(q, k, vt):
    s = k.shape[0]
    tq = ATTN_TQ
    assert ATTN_TQ == ATTN_TK and vt.shape == (s // ATTN_TK, LANES * N_PAIRS, ATTN_TK)
    return pl.pallas_call(
        _mla_prompt_attn_kernel,
        grid=(N_PAIRS, s // tq),
        in_specs=[pl.BlockSpec((tq, 2 * LANES), lambda j, i: (i, j)),
                  pl.BlockSpec((s, 2 * LANES), lambda j, i: (0, j)),
                  pl.BlockSpec((s // ATTN_TK, LANES, ATTN_TK), lambda j, i: (0, j, 0))],
        out_specs=pl.BlockSpec((tq, LANES), lambda j, i: (i, j)),
        out_shape=jax.ShapeDtypeStruct((s, LANES * N_PAIRS), BF16),
        scratch_shapes=[pltpu.VMEM((2, tq, 2 * LANES), BF16),
                        pltpu.VMEM((2, 2, ATTN_TK, tq), F32),
                        pltpu.VMEM((2, 2, 1, tq), F32),
                        pltpu.VMEM((2, 1, tq), F32),
                        pltpu.VMEM((2, LANES // 2 + ONES_ROWS, tq), F32)],
        compiler_params=_params("parallel", "arbitrary"),
        name="mla_prompt_attn",
    )(q, k, vt)


def _static_visible(q_pos, k_pos, left_chunks=None):
    qc = (q_pos // CHUNK)[:, None]
    kc = (k_pos // CHUNK)[None, :]
    vis = kc <= qc
    if left_chunks is not None:
        vis = vis & (kc >= qc - left_chunks) & (k_pos[None, :] >= 0)
    return vis


def _mla_sample_attn_kernel(q_ref, cc_ref, rc_ref, cn_ref, rn_ref, wukt_ref, wuv_ref, o_ref, *,
                            visible):
    nope_masks = _pair_query_masks(2 * LANES)
    lane = lax.broadcasted_iota(jnp.int32, (1, LANES), 1)
    low = lane < LANES // 2
    q_lat, q_rope = [], []
    for j in range(N_PAIRS):
        q = q_ref[0, :, 2 * LANES * j:2 * LANES * (j + 1)]
        for h in range(2):
            qm = jnp.where(nope_masks[h], q, jnp.zeros_like(q))
            q_lat.append(_dot(qm[:, :LANES], wukt_ref[LANES * j:LANES * (j + 1), :]).astype(BF16))
            q_rope.append(qm[:, LANES:])
    q_lat = jnp.concatenate(q_lat, axis=0)
    q_rope = jnp.concatenate(q_rope, axis=0)
    cc = cc_ref[0].astype(BF16)
    cn = cn_ref[0].astype(BF16)
    s_c = _dot_nt(q_lat, cc) + _dot_nt(q_rope, rc_ref[0])
    s_n = _dot_nt(q_lat, cn) + _dot_nt(q_rope, rn_ref[0].astype(BF16))
    if visible is not None:
        vis_c, vis_n = (jnp.asarray(np.tile(v, (MLA_HEADS, 1))) for v in visible)
        s_c = jnp.where(vis_c, s_c, NEG_INF)
        s_n = jnp.where(vis_n, s_n, NEG_INF)
    m = jnp.maximum(jnp.max(s_c, -1, keepdims=True), jnp.max(s_n, -1, keepdims=True))
    p_c = jnp.exp2(s_c - m)
    p_n = jnp.exp2(s_n - m)
    l = jnp.sum(p_c, -1, keepdims=True) + jnp.sum(p_n, -1, keepdims=True)
    o_lat = ((_dot(p_c.astype(BF16), cc) + _dot(p_n.astype(BF16), cn)) / l).astype(BF16)
    t = q_ref.shape[1]
    for j in range(N_PAIRS):
        w = wuv_ref[:, LANES * j:LANES * (j + 1)]
        o0 = _dot(o_lat[2 * j * t:(2 * j + 1) * t], w)
        o1 = _dot(o_lat[(2 * j + 1) * t:(2 * j + 2) * t], w)
        o_ref[0, :, LANES * j:LANES * (j + 1)] = jnp.where(low, o0, o1).astype(o_ref.dtype)


def _mla_sample_attn(q, cache_ckv, cache_kr128, new_ckv, new_kr128, wuk_t, wuv):
    b, t, _ = q.shape
    n_past = cache_ckv.shape[1]
    q_pos = PAST_LEN + np.arange(t)
    vis_c = _static_visible(q_pos, PAST_LEN - n_past + np.arange(n_past))
    vis_n = _static_visible(q_pos, q_pos)
    visible = None if (vis_c.all() and vis_n.all()) else (vis_c, vis_n)
    blk = lambda a: pl.BlockSpec((1,) + a.shape[1:], lambda i: (i, 0, 0))
    full = lambda a: pl.BlockSpec(a.shape, lambda i: (0, 0))
    return pl.pallas_call(
        functools.partial(_mla_sample_attn_kernel, visible=visible),
        grid=(b,),
        in_specs=[blk(q), blk(cache_ckv), blk(cache_kr128), blk(new_ckv), blk(new_kr128),
                  full(wuk_t), full(wuv)],
        out_specs=pl.BlockSpec((1, t, LANES * N_PAIRS), lambda i: (i, 0, 0)),
        out_shape=jax.ShapeDtypeStruct((b, t, LANES * N_PAIRS), BF16),
        compiler_params=_params("parallel"),
        name="mla_sample_attn",
    )(q, cache_ckv, cache_kr128, new_ckv, new_kr128, wuk_t, wuv)


def _out_ln_kernel(op_ref, os_ref, w_ref, *refs, prompt_tiles):
    *x_refs, g_ref, b_ref, y_ref = refs
    x = x_refs[0][...] if len(x_refs) == 1 else _stream_tile(*x_refs, prompt_tiles)
    y = DEEPNORM_ALPHA * x + _dot(_stream_tile(op_ref, os_ref, prompt_tiles), w_ref[...])
    y_ref[...] = _layer_norm(y, g_ref[...], b_ref[...])


def _out_ln(o_p, o_s, w_o, x, g, b):
    tm = TOKEN_TILE
    n_p = o_p.shape[0] // tm
    n = o_p.shape[0] + o_s.shape[0]
    row = pl.BlockSpec((tm, D_MODEL), lambda i: (i, 0))
    full = lambda a: pl.BlockSpec(a.shape, lambda i: (0, 0))
    xs = list(x) if isinstance(x, tuple) else [x]
    x_specs = _stream_specs(tm, D_MODEL, n_p) if isinstance(x, tuple) else [row]
    return pl.pallas_call(
        functools.partial(_out_ln_kernel, prompt_tiles=n_p),
        grid=(n // tm,),
        in_specs=_stream_specs(tm, D_MODEL, n_p) + [full(w_o)] + x_specs + [full(g), full(b)],
        out_specs=row,
        out_shape=jax.ShapeDtypeStruct((n, D_MODEL), F32),
        compiler_params=_params("parallel"),
        name="out_proj_ln",
    )(o_p, o_s, w_o, *xs, g, b)


def _band_qkv_kernel(x_ref, wqk_ref, wv_ref, wvt_ref, q_ref, k_ref, vt_ref, kf_ref, vf_ref, *,
                     first_keep_tile):
    xb = x_ref[...].astype(BF16)
    qk = _dot(xb, wqk_ref[...])
    scale = math.log2(math.e) / math.sqrt(BAND_HEAD_DIM)
    q_ref[...] = (qk[:, :D_MODEL] * scale).astype(BF16)
    k = qk[:, D_MODEL:]
    k_ref[...] = k.astype(BF16)
    vt = _dot_nt(wvt_ref[...], xb).astype(BF16)
    for t in range(vt_ref.shape[0]):
        vt_ref[t] = vt[:, LANES * t:LANES * (t + 1)]

    @pl.when(pl.program_id(0) >= first_keep_tile)
    def _():
        kf_ref[...] = k
        vf_ref[...] = _dot(xb, wv_ref[...])


def _band_qkv(x, w_qkv, n_keep):
    n = x.shape[0]
    tm = TOKEN_TILE
    first = (n - n_keep) // tm
    w = w_qkv.astype(BF16)
    wqk, wv = w[:, :2 * D_MODEL], w[:, 2 * D_MODEL:]
    row = pl.BlockSpec((tm, D_MODEL), lambda i: (i, 0))
    keep = pl.BlockSpec((tm, D_MODEL), lambda i: (jnp.maximum(i - first, 0), 0))
    full = lambda a: pl.BlockSpec(a.shape, lambda i: (0, 0))
    bf = jax.ShapeDtypeStruct((n, D_MODEL), BF16)
    kf = jax.ShapeDtypeStruct((n_keep, D_MODEL), F32)
    return pl.pallas_call(
        functools.partial(_band_qkv_kernel, first_keep_tile=first),
        grid=(n // tm,),
        in_specs=[row, full(wqk), full(wv), full(wv)],
        out_specs=[row, row, pl.BlockSpec((tm // LANES, D_MODEL, LANES), lambda i: (i, 0, 0)),
                   keep, keep],
        out_shape=[bf, bf, jax.ShapeDtypeStruct((n // LANES, D_MODEL, LANES), BF16), kf, kf],
        compiler_params=_params("arbitrary"),
        name="band_qkv",
    )(x, wqk, wv, wv.T)


def _band_prompt_attn_kernel(q_ref, k_ref, vt_ref, rel_ref, o_ref, bias_sc):
    i = pl.program_id(1)
    half = LANES // 2
    low = lax.broadcasted_iota(jnp.int32, (1, LANES), 1) < half
    n_win = BAND_WIN // LANES
    n_left = BAND_KEEP_MAX // LANES

    @pl.when(i == 0)
    def _():
        k_in = lax.broadcasted_iota(jnp.int32, (LANES, LANES), 0)
        q_in = lax.broadcasted_iota(jnp.int32, (LANES, LANES), 1)
        for t in range(n_left + 1):
            q_chunk = (BAND_QSUB * t + q_in) // CHUNK
            for kb in range(n_win):
                k_chunk = (LANES * kb + k_in) // CHUNK
                visible = (k_chunk <= q_chunk) & (k_chunk >= q_chunk - LEFT_CHUNKS)
                start = LANES * (t - kb + n_left)
                for h in range(2):
                    run = jnp.broadcast_to(rel_ref[h, :, start:start + 2 * LANES], (LANES, 2 * LANES))
                    tile = pltpu.roll(run, 0, 1, stride=1, stride_axis=0)[:, LANES:]
                    bias_sc[t, LANES * kb:LANES * (kb + 1), LANES * h:LANES * (h + 1)] = jnp.where(
                        visible, tile, NEG_INF)

    def body(g, c):
        work = []
        for u in range(BAND_GROUP):
            sb = g * BAND_GROUP + u
            blk = i * (BAND_QBLOCK // BAND_QSUB) + sb
            b0 = jnp.maximum(blk - n_left, 0)
            rows = pl.ds(pl.multiple_of(sb * BAND_QSUB, BAND_QSUB), BAND_QSUB)
            q = q_ref[rows, :]
            zero = jnp.zeros_like(q)
            qq = jnp.concatenate([jnp.where(low, q, zero), jnp.where(low, zero, q)], axis=0)
            kw = k_ref[pl.ds(pl.multiple_of(b0 * LANES, LANES), BAND_WIN), :]
            s = _dot_nt(kw, qq) + bias_sc[jnp.minimum(blk, n_left)]
            work.append((rows, b0, s))
        for rows, b0, s in work:
            m = jnp.max(s, 0, keepdims=True)
            p = jnp.exp2(s - m)
            l = jnp.sum(p, 0, keepdims=True)
            pb = p.astype(BF16)
            vt = jnp.concatenate([vt_ref[b0 + w] for w in range(n_win)], axis=1)
            o_t = jnp.concatenate(
                [_dot(vt[half * h:half * (h + 1)], pb[:, LANES * h:LANES * (h + 1)])
                 / l[:, LANES * h:LANES * (h + 1)] for h in range(2)], axis=0)
            o_ref[rows, :] = o_t.T.astype(o_ref.dtype)
        return c

    lax.fori_loop(0, BAND_QBLOCK // BAND_QSUB // BAND_GROUP, body, 0)


def _band_prompt_attn(q, k, vt, rel_table, s):
    n_left = BAND_KEEP_MAX // LANES
    offs = np.arange(LANES * (2 * n_left + 2)) - LANES * (n_left + 1)
    rel_run = (rel_table.astype(F32)[:, np.clip(offs, -REL_MAX, REL_MAX) + REL_MAX]
               * math.log2(math.e))[:, None, :]
    return pl.pallas_call(
        _band_prompt_attn_kernel,
        grid=(N_PAIRS, s // BAND_QBLOCK),
        in_specs=[pl.BlockSpec((BAND_QBLOCK, LANES), lambda j, i: (i, j)),
                  pl.BlockSpec((s, LANES), lambda j, i: (0, j)),
                  pl.BlockSpec((s // LANES, LANES, LANES), lambda j, i: (0, j, 0)),
                  pl.BlockSpec((2, 1, rel_run.shape[-1]), lambda j, i: (j, 0, 0))],
        out_specs=pl.BlockSpec((BAND_QBLOCK, LANES), lambda j, i: (i, j)),
        out_shape=jax.ShapeDtypeStruct((s, D_MODEL), BF16),
        scratch_shapes=[pltpu.VMEM((n_left + 1, BAND_WIN, 2 * LANES), F32)],
        compiler_params=_params("parallel", "arbitrary"),
        name="band_prompt_attn",
    )(q, k, vt, rel_run)


def _band_sample_attn_kernel(q_ref, kc_ref, vc_ref, kn_ref, vn_ref, bc_ref, bn_ref, o_ref):
    lane = lax.broadcasted_iota(jnp.int32, (1, LANES), 1)
    low = lane < LANES // 2
    for j in range(N_PAIRS):
        cols = slice(LANES * j, LANES * (j + 1))
        q = q_ref[0, :, cols]
        kc = kc_ref[0, :, cols]
        vc = vc_ref[0, :, cols]
        kn = kn_ref[0, :, cols].astype(BF16)
        vn = vn_ref[0, :, cols].astype(BF16)
        outs = []
        for h in range(2):
            qh = jnp.where(low if h == 0 else ~low, q, jnp.zeros_like(q))
            sc = _dot_nt(qh, kc) + bc_ref[2 * j + h]
            sn = _dot_nt(qh, kn) + bn_ref[2 * j + h]
            m = jnp.maximum(jnp.max(sc, -1, keepdims=True), jnp.max(sn, -1, keepdims=True))
            pc = jnp.exp2(sc - m)
            pn = jnp.exp2(sn - m)
            l = jnp.sum(pc, -1, keepdims=True) + jnp.sum(pn, -1, keepdims=True)
            outs.append((_dot(pc.astype(BF16), vc) + _dot(pn.astype(BF16), vn)) / l)
        o_ref[0, :, cols] = jnp.where(low, outs[0], outs[1]).astype(o_ref.dtype)


def _band_sample_attn(q, kc, vc, kn, vn, bias_c, bias_n):
    b, t, _ = q.shape
    blk = lambda a: pl.BlockSpec((1,) + a.shape[1:], lambda i: (i, 0, 0))
    full = lambda a: pl.BlockSpec(a.shape, lambda i: (0, 0, 0))
    return pl.pallas_call(
        _band_sample_attn_kernel,
        grid=(b,),
        in_specs=[blk(q), blk(kc), blk(vc), blk(kn), blk(vn), full(bias_c), full(bias_n)],
        out_specs=pl.BlockSpec((1, t, D_MODEL), lambda i: (i, 0, 0)),
        out_shape=jax.ShapeDtypeStruct((b, t, D_MODEL), BF16),
        compiler_params=_params("parallel"),
        name="band_sample_attn",
    )(q, kc, vc, kn, vn, bias_c, bias_n)


def _band_bias(rel_table, q_pos, k_pos):
    nq, nk = len(q_pos), len(k_pos)
    assert (np.diff(q_pos) == 1).all() and (np.diff(k_pos) == 1).all()
    span = nq + nk - 1
    diag = int(q_pos[0] - k_pos[0]) + nq - 1 - np.arange(span)
    u = rel_table.astype(F32)[:, np.clip(diag, -REL_MAX, REL_MAX) + REL_MAX] * math.log2(math.e)
    u = jnp.pad(u, ((0, 0), (0, 1)))
    skew = jnp.tile(u, (1, nq))[:, :nq * span].reshape(-1, nq, span)
    bias = skew[:, :, nq - 1:nq - 1 + nk]
    vis = _static_visible(q_pos, k_pos, LEFT_CHUNKS)
    return jnp.where(jnp.asarray(vis)[None], bias, NEG_INF)


def _route(logits_t, rb_ref):
    scores = jax.nn.sigmoid(logits_t)
    rows = [scores[e:e + 1, :] for e in range(N_EXPERTS)]
    biased = [rows[e] + rb_ref[e] for e in range(N_EXPERTS)]
    best = None
    sel = None
    for g in range(N_GROUPS):
        a, b, c, d = biased[EXPERTS_PER_GROUP * g:EXPERTS_PER_GROUP * (g + 1)]
        top2 = jnp.maximum(jnp.maximum(jnp.maximum(a + b, a + c), jnp.maximum(a + d, b + c)),
                           jnp.maximum(b + d, c + d))
        if g == 0:
            best, sel = top2, jnp.zeros(top2.shape, jnp.int32)
        else:
            better = top2 > best
            sel = jnp.where(better, g, sel)
            best = jnp.where(better, top2, best)
    cand = [jnp.where(sel == e // EXPERTS_PER_GROUP, biased[e], NEG_INF) for e in range(N_EXPERTS)]

    def argmax_first(vals):
        top = functools.reduce(jnp.maximum, vals)
        idx = jnp.full(top.shape, N_EXPERTS, jnp.int32)
        for e in reversed(range(N_EXPERTS)):
            idx = jnp.where(vals[e] == top, e, idx)
        return idx

    i1 = argmax_first(cand)
    i2 = argmax_first([jnp.where(i1 == e, -jnp.inf, cand[e]) for e in range(N_EXPERTS)])
    picked = [jnp.where((i1 == e) | (i2 == e), rows[e], 0.0) for e in range(N_EXPERTS)]
    total = functools.reduce(jnp.add, picked)
    return [p / total for p in picked]


def _moe_kernel(x_ref, rw_ref, rb_ref, wg_ref, wu_ref, wd_ref, g_ref, b_ref, *refs, prompt_tiles):
    *y_refs, xb_sc, comb_sc, acc_sc = refs
    grp = pl.program_id(1)

    @pl.when(grp == 0)
    def _():
        x = x_ref[...]
        xb = x.astype(BF16)
        xb_sc[...] = xb
        acc_sc[...] = DEEPNORM_ALPHA * x
        x_lo = (x - xb.astype(F32)).astype(BF16)
        rw_hi = rw_ref[0]
        logits_t = _dot_nt(rw_hi, xb) + (_dot_nt(rw_hi, x_lo) + _dot_nt(rw_ref[1], xb))
        comb_rows = _route(logits_t, rb_ref)
        comb = jnp.concatenate(comb_rows, axis=0).T
        for g in range(N_GROUPS):
            comb_sc[g] = comb[:, EXPERTS_PER_GROUP * g:EXPERTS_PER_GROUP * (g + 1)]

    xb = xb_sc[...]
    comb = comb_sc[grp]
    parts = []
    for e in range(EXPERTS_PER_GROUP):
        h = jax.nn.silu(_dot(xb, wg_ref[e])) * _dot(xb, wu_ref[e])
        parts.append((h * comb[:, e:e + 1]).astype(BF16))
    acc_sc[...] += _dot(jnp.concatenate(parts, axis=1), wd_ref[0])

    @pl.when(grp == N_GROUPS - 1)
    def _():
        y = _layer_norm(acc_sc[...], g_ref[...], b_ref[...])
        if len(y_refs) == 1:
            y_refs[0][...] = y
        else:
            is_prompt = pl.program_id(0) < prompt_tiles

            @pl.when(is_prompt)
            def _():
                y_refs[0][...] = y

            @pl.when(jnp.logical_not(is_prompt))
            def _():
                y_refs[1][...] = y


def _moe_ln(x, rw_t, rb, wg, wu, wd, layer, g, b, n_prompt=None):
    n = x.shape[0]
    tm = MOE_TILE
    row = pl.BlockSpec((tm, D_MODEL), lambda i, j: (i, 0))
    full = lambda a: pl.BlockSpec(a.shape, lambda i, j: (0, 0))
    grp = lambda a: pl.BlockSpec((None, a.shape[1] // N_GROUPS) + a.shape[2:],
                                 lambda i, j: (layer, j, 0, 0))
    if n_prompt is None:
        n_p = n // tm
        out_specs = row
        out_shape = jax.ShapeDtypeStruct((n, D_MODEL), F32)
        semantics = ("parallel", "arbitrary")
    else:
        n_p = n_prompt // tm
        out_specs = _stream_specs(tm, D_MODEL, n_p)
        out_shape = [jax.ShapeDtypeStruct((n_prompt, D_MODEL), F32),
                     jax.ShapeDtypeStruct((n - n_prompt, D_MODEL), F32)]
        semantics = ("arbitrary", "arbitrary")
    return pl.pallas_call(
        functools.partial(_moe_kernel, prompt_tiles=n_p),
        grid=(n // tm, N_GROUPS),
        in_specs=[row, pl.BlockSpec(rw_t.shape, lambda i, j: (0, 0, 0)),
                  pl.BlockSpec(memory_space=pltpu.SMEM),
                  grp(wg), grp(wu), grp(wd), full(g), full(b)],
        out_specs=out_specs,
        out_shape=out_shape,
        scratch_shapes=[pltpu.VMEM((tm, D_MODEL), BF16),
                        pltpu.VMEM((N_GROUPS, tm, EXPERTS_PER_GROUP), F32),
                        pltpu.VMEM((tm, D_MODEL), F32)],
        compiler_params=_params(*semantics),
        name="moe_ln",
    )(x, rw_t, rb, wg, wu, wd, g, b)


def _rope_tables():
    half = MLA_ROPE // 2
    inv = ROPE_THETA ** (-jnp.arange(half, dtype=F32) / half)
    pos = jnp.concatenate([jnp.arange(SEQ, dtype=jnp.int32),
                           jnp.tile(PAST_LEN + jnp.arange(DEC_SEQ, dtype=jnp.int32), DEC_BATCH)])
    ang = pos.astype(F32)[:, None] * inv[None, :]
    reps = LANES // half
    return jnp.tile(jnp.cos(ang), (1, reps)), jnp.tile(jnp.sin(ang), (1, reps))


def _rot_cols(w):
    half = MLA_ROPE // 2
    return jnp.concatenate([-w[..., half:], w[..., :half]], -1)


def _mla_weights(w_in, w_uq, w_uk, w_uv, w_o):
    o_kv = MLA_Q_LORA + MLA_KV_LORA
    w_r = w_in[:, o_kv:]
    zpad = jnp.zeros((D_MODEL, LANES - 2 * MLA_ROPE), F32)
    w_in_p = jnp.concatenate([w_in[:, :o_kv], w_r, w_r, zpad,
                              _rot_cols(w_r), _rot_cols(w_r), zpad], 1).astype(BF16)
    wq = w_uq.reshape(MLA_Q_LORA, N_PAIRS, 2, MLA_QK)
    nope = wq[..., :MLA_NOPE].reshape(MLA_Q_LORA, N_PAIRS, 2 * MLA_NOPE)
    ropew = wq[..., MLA_NOPE:]
    zq = jnp.zeros((MLA_Q_LORA, N_PAIRS, LANES - 2 * MLA_ROPE), F32)
    rope_cols = jnp.concatenate([ropew.reshape(MLA_Q_LORA, N_PAIRS, 2 * MLA_ROPE), zq], -1)
    rot_cols = jnp.concatenate([_rot_cols(ropew).reshape(MLA_Q_LORA, N_PAIRS, 2 * MLA_ROPE), zq], -1)
    wqa = jnp.concatenate([nope, rope_cols], -1).reshape(MLA_Q_LORA, 2 * LANES * N_PAIRS).astype(BF16)
    wqb = rot_cols.reshape(MLA_Q_LORA, LANES * N_PAIRS).astype(BF16)
    return w_in_p, wqa, wqb, w_uk.astype(BF16), w_uv.astype(BF16), w_o.astype(BF16)


def _group_experts(w_gate, w_up, w_down):
    wd = w_down.reshape(DEPTH, N_GROUPS, EXPERTS_PER_GROUP * D_EXPERT, D_MODEL).astype(BF16)
    return w_gate.astype(BF16), w_up.astype(BF16), wd


def kernel(x_prompt, x_sample, cache_mla_ckv, cache_mla_krope, cache_band_k, cache_band_v,
           mla_w_in, mla_q_norm, mla_kv_norm, mla_w_uq, mla_w_uk, mla_w_uv, mla_w_o,
           band_w_qkv, band_rel_bias, band_w_o,
           router_w, router_b, moe_w_gate, moe_w_up, moe_w_down,
           ln_mix_g, ln_mix_b, ln_ffn_g, ln_ffn_b):
    x = (x_prompt.reshape(SEQ, D_MODEL), x_sample.reshape(N_SAMPLE, D_MODEL))
    wg, wu, wd = _group_experts(moe_w_gate, moe_w_up, moe_w_down)
    cos, sin = _rope_tables()
    rw_f = router_w.T.astype(F32)
    rw_hi = rw_f.astype(BF16)
    rw_t = jnp.stack([rw_hi, (rw_f - rw_hi.astype(F32)).astype(BF16)], 0)
    rb = router_b.astype(F32)
    band_keep = cache_band_k.shape[2]
    row2 = lambda a: a.reshape(1, -1)

    ckv_p, kr_p, ckv_s, kr_s = [], [], [], []
    bk_p, bv_p, bk_s, bv_s = [], [], [], []
    for i in range(DEPTH):
        j = i // N_MIXERS
        if i % N_MIXERS == 0:
            w_in_p, wqa, wqb, wuk, wuv, w_o = _mla_weights(
                mla_w_in[j], mla_w_uq[j], mla_w_uk[j], mla_w_uv[j], mla_w_o[j])
            x_pair = x if isinstance(x, tuple) else (x[:SEQ], x[SEQ:])
            ckv, kr128, q = _mla_proj(*x_pair, w_in_p, row2(mla_q_norm[j]), row2(mla_kv_norm[j]),
                                      wqa, wqb, cos, sin)
            kr = kr128[:, :MLA_ROPE]
            ckv_p.append(ckv[:SEQ].reshape(1, SEQ, MLA_KV_LORA))
            kr_p.append(kr[:SEQ].reshape(1, SEQ, MLA_ROPE))
            ckv_s.append(ckv[SEQ:].reshape(DEC_BATCH, DEC_SEQ, MLA_KV_LORA))
            kr_s.append(kr[SEQ:].reshape(DEC_BATCH, DEC_SEQ, MLA_ROPE))
            k_p, vt_p = _kv_expand(ckv, kr128, wuk, wuv.T, SEQ)
            o_p = _mla_prompt_attn(q, k_p, vt_p)
            cache_kr = cache_mla_krope[j].astype(BF16)
            cache_kr128 = jnp.concatenate(
                [cache_kr, cache_kr, jnp.zeros(cache_kr.shape[:2] + (LANES - 2 * MLA_ROPE,), BF16)], -1)
            o_s = _mla_sample_attn(q[SEQ:].reshape(DEC_BATCH, DEC_SEQ, -1), cache_mla_ckv[j], cache_kr128,
                                   ckv_s[-1], kr128[SEQ:].reshape(DEC_BATCH, DEC_SEQ, LANES), wuk.T, wuv)
        else:
            n_keep = min(BAND_KEEP_MAX, SEQ) + N_SAMPLE
            x_all = jnp.concatenate(x, 0) if isinstance(x, tuple) else x
            q, k, vt, k_f, v_f = _band_qkv(x_all, band_w_qkv[j], n_keep)
            keep = n_keep - N_SAMPLE
            hd = (BAND_HEADS, BAND_HEAD_DIM)
            bk_p.append(k_f[:keep].reshape((1, keep) + hd))
            bv_p.append(v_f[:keep].reshape((1, keep) + hd))
            bk_s.append(k_f[keep:].reshape((DEC_BATCH, DEC_SEQ) + hd))
            bv_s.append(v_f[keep:].reshape((DEC_BATCH, DEC_SEQ) + hd))
            w_o = band_w_o[j].astype(BF16)
            o_p = _band_prompt_attn(q, k, vt, band_rel_bias[j], SEQ)
            q_pos = PAST_LEN + np.arange(DEC_SEQ)
            pos_c = PAST_LEN - band_keep + np.arange(band_keep)
            bias_c = _band_bias(band_rel_bias[j], q_pos, pos_c)
            bias_n = _band_bias(band_rel_bias[j], q_pos, q_pos)
            s3 = lambda a: a.reshape(DEC_BATCH, DEC_SEQ, D_MODEL)
            cache3 = lambda a: a.astype(BF16).reshape(DEC_BATCH, band_keep, D_MODEL)
            o_s = _band_sample_attn(s3(q[SEQ:]), cache3(cache_band_k[j]), cache3(cache_band_v[j]),
                                    s3(k_f[keep:]), s3(v_f[keep:]), bias_c, bias_n)
        x = _out_ln(o_p, o_s.reshape(N_SAMPLE, D_MODEL), w_o, x, row2(ln_mix_g[i]), row2(ln_mix_b[i]))
        x = _moe_ln(x, rw_t, rb, wg, wu, wd, i, row2(ln_ffn_g[i]), row2(ln_ffn_b[i]),
                    n_prompt=SEQ if i == DEPTH - 1 else None)

    return (x[0].reshape(1, SEQ, D_MODEL), x[1].reshape(DEC_BATCH, DEC_SEQ, D_MODEL),
            jnp.stack(ckv_p, 0), jnp.stack(kr_p, 0), jnp.stack(ckv_s, 0), jnp.stack(kr_s, 0),
            jnp.stack(bk_p, 0), jnp.stack(bv_p, 0), jnp.stack(bk_s, 0), jnp.stack(bv_s, 0))
```

```python
import functools
import math

import numpy as np
import jax
import jax.numpy as jnp
from jax import lax
from jax.experimental import pallas as pl
from jax.experimental.pallas import tpu as pltpu

D_MODEL = 1024
SEQ = 16384
DEPTH = 2
DEC_BATCH = 32
DEC_SEQ = 32
PAST_LEN = 1024
CHUNK = 64
N_MIXERS = 2

MLA_HEADS = 16
MLA_Q_LORA = 384
MLA_KV_LORA = 256
MLA_NOPE = 64
MLA_ROPE = 32
MLA_QK = MLA_NOPE + MLA_ROPE
MLA_V = 64
ROPE_THETA = 10000.0

BAND_HEADS = 16
BAND_HEAD_DIM = D_MODEL // BAND_HEADS
LEFT_CHUNKS = 8
BAND_KEEP_MAX = LEFT_CHUNKS * CHUNK
REL_MAX = 128

N_EXPERTS = 16
N_GROUPS = 4
EXPERTS_PER_GROUP = N_EXPERTS // N_GROUPS
D_EXPERT = 256

DEEPNORM_ALPHA = (2.0 * DEPTH) ** 0.25
NORM_EPS = 1e-5
NEG_INF = -1e30

N_SAMPLE = DEC_BATCH * DEC_SEQ
N_TOKENS = SEQ + N_SAMPLE
N_PAIRS = MLA_HEADS // 2
LANES = 128
VMEM_LIMIT = 56 * 1024 * 1024

TOKEN_TILE = 512
MOE_TILE = 1024
ATTN_TQ = 512
ATTN_TK = 512
ONES_ROWS = 16
BAND_QSUB = 128
BAND_WIN = BAND_KEEP_MAX + BAND_QSUB
BAND_QBLOCK = 1024
BAND_GROUP = 4

BF16 = jnp.bfloat16
F32 = jnp.float32


def _params(*sem):
    return pltpu.CompilerParams(dimension_semantics=sem, vmem_limit_bytes=VMEM_LIMIT)


def _dot(a, b):
    return jnp.dot(a, b, preferred_element_type=F32)


def _dot_nt(a, b):
    return lax.dot_general(a, b, (((1,), (1,)), ((), ())), preferred_element_type=F32)


def _layer_norm(y, g, b):
    mu = jnp.mean(y, -1, keepdims=True)
    d = y - mu
    var = jnp.mean(d * d, -1, keepdims=True)
    return d * lax.rsqrt(var + NORM_EPS) * g + b


def _rms_norm(y, g):
    return y * lax.rsqrt(jnp.mean(y * y, -1, keepdims=True) + NORM_EPS) * g


def _stream_specs(tm, width, prompt_tiles):
    return [pl.BlockSpec((tm, width), lambda i, *_: (jnp.minimum(i, prompt_tiles - 1), 0)),
            pl.BlockSpec((tm, width), lambda i, *_: (jnp.maximum(i - prompt_tiles, 0), 0))]


def _stream_tile(p_ref, s_ref, prompt_tiles):
    return jnp.where(pl.program_id(0) < prompt_tiles, p_ref[...], s_ref[...])


def _mla_proj_kernel(xp_ref, xs_ref, w_in_ref, qn_ref, kvn_ref, wqa_ref, wqb_ref, cos_ref, sin_ref,
                     ckv_ref, kr_ref, q_ref, *, prompt_tiles):
    xb = _stream_tile(xp_ref, xs_ref, prompt_tiles).astype(BF16)
    a = _dot(xb, w_in_ref[...])
    cq = _rms_norm(a[:, :MLA_Q_LORA], qn_ref[...])
    o_kv = MLA_Q_LORA + MLA_KV_LORA
    ckv_ref[...] = _rms_norm(a[:, MLA_Q_LORA:o_kv], kvn_ref[...])
    cos = cos_ref[...]
    sin = sin_ref[...]
    kr_ref[...] = a[:, o_kv:o_kv + LANES] * cos + a[:, o_kv + LANES:] * sin
    cqb = cq.astype(BF16)
    qa = _dot(cqb, wqa_ref[...])
    qb = _dot(cqb, wqb_ref[...])
    scale = math.log2(math.e) / math.sqrt(MLA_QK)
    for j in range(N_PAIRS):
        c0 = 2 * LANES * j
        q_ref[:, c0:c0 + LANES] = (qa[:, c0:c0 + LANES] * scale).astype(BF16)
        rot = qa[:, c0 + LANES:c0 + 2 * LANES] * cos + qb[:, LANES * j:LANES * (j + 1)] * sin
        q_ref[:, c0 + LANES:c0 + 2 * LANES] = (rot * scale).astype(BF16)


def _mla_proj(x_p, x_s, w_in_p, qn, kvn, wqa, wqb, cos, sin):
    n = x_p.shape[0] + x_s.shape[0]
    tm = TOKEN_TILE
    n_p = x_p.shape[0] // tm
    row = lambda w: pl.BlockSpec((tm, w), lambda i: (i, 0))
    full = lambda a: pl.BlockSpec(a.shape, lambda i: (0, 0))
    return pl.pallas_call(
        functools.partial(_mla_proj_kernel, prompt_tiles=n_p),
        grid=(n // tm,),
        in_specs=_stream_specs(tm, D_MODEL, n_p) + [full(w_in_p), full(qn), full(kvn), full(wqa),
                                                    full(wqb), row(LANES), row(LANES)],
        out_specs=[row(MLA_KV_LORA), row(LANES), row(2 * LANES * N_PAIRS)],
        out_shape=[jax.ShapeDtypeStruct((n, MLA_KV_LORA), F32),
                   jax.ShapeDtypeStruct((n, LANES), F32),
                   jax.ShapeDtypeStruct((n, 2 * LANES * N_PAIRS), BF16)],
        compiler_params=_params("parallel"),
        name="mla_proj",
    )(x_p, x_s, w_in_p, qn, kvn, wqa, wqb, cos, sin)


def _kv_expand_kernel(ckv_ref, kr_ref, wuk_ref, wuvt_ref, k_ref, vt_ref):
    cb = ckv_ref[...].astype(BF16)
    kn = _dot(cb, wuk_ref[...])
    vt_ref[0] = _dot_nt(wuvt_ref[...], cb).astype(BF16)
    krb = kr_ref[...].astype(BF16)
    for j in range(N_PAIRS):
        c0 = 2 * LANES * j
        k_ref[:, c0:c0 + LANES] = kn[:, LANES * j:LANES * (j + 1)].astype(BF16)
        k_ref[:, c0 + LANES:c0 + 2 * LANES] = krb


def _kv_expand(ckv, kr128, wuk, wuv_t, n):
    tm = ATTN_TK
    row = lambda w: pl.BlockSpec((tm, w), lambda i: (i, 0))
    full = lambda a: pl.BlockSpec(a.shape, lambda i: (0, 0))
    nv = LANES * N_PAIRS
    return pl.pallas_call(
        _kv_expand_kernel,
        grid=(n // tm,),
        in_specs=[row(MLA_KV_LORA), row(LANES), full(wuk), full(wuv_t)],
        out_specs=[row(2 * LANES * N_PAIRS), pl.BlockSpec((1, nv, tm), lambda i: (i, 0, 0))],
        out_shape=[jax.ShapeDtypeStruct((n, 2 * LANES * N_PAIRS), BF16),
                   jax.ShapeDtypeStruct((n // tm, nv, tm), BF16)],
        compiler_params=_params("parallel"),
        name="mla_kv_expand",
    )(ckv, kr128, wuk, wuv_t)


def _pair_query_masks(width):
    lane = lax.broadcasted_iota(jnp.int32, (1, width), 1)
    half = LANES // 2
    r0 = LANES
    m0 = (lane < half) | ((lane >= r0) & (lane < r0 + MLA_ROPE))
    m1 = ((lane >= half) & (lane < LANES)) | ((lane >= r0 + MLA_ROPE) & (lane < r0 + 2 * MLA_ROPE))
    return m0, m1


def _mla_prompt_attn_kernel(q_ref, k_ref, vt_ref, o_ref, qh_sc, s_sc, bmax_sc, m_sc, acc_sc):
    i = pl.program_id(1)
    tq, tk = ATTN_TQ, ATTN_TK
    half = LANES // 2
    q = q_ref[...]
    for h, mask in enumerate(_pair_query_masks(2 * LANES)):
        qh_sc[h] = jnp.where(mask, q, jnp.zeros_like(q))
        m_sc[h] = jnp.full((1, tq), NEG_INF, F32)
        acc_sc[h] = jnp.zeros((half + ONES_ROWS, tq), F32)
    ones = jnp.ones((ONES_ROWS, tk), BF16)

    def scores(kb, h, slot):
        k = k_ref[pl.ds(pl.multiple_of(kb * tk, tk), tk), :]
        s = _dot_nt(k, qh_sc[h])
        s_sc[h, slot] = s
        bmax_sc[h, slot] = jnp.max(s, 0, keepdims=True)

    def update(kb, h, slot, visible=None):
        s = s_sc[h, slot]
        if visible is None:
            block_max = bmax_sc[h, slot]
        else:
            s = jnp.where(visible, s, NEG_INF)
            block_max = jnp.max(s, 0, keepdims=True)
        m_prev = m_sc[h]
        m_new = jnp.maximum(m_prev, block_max)
        alpha = jnp.exp2(m_prev - m_new)
        p = jnp.exp2(s - m_new).astype(BF16)
        vt = jnp.concatenate([vt_ref[kb, half * h:half * (h + 1), :], ones], axis=0)
        acc_sc[h] = alpha * acc_sc[h] + _dot(vt, p)
        m_sc[h] = m_new

    def step(next_kb, kb, slot, visible=None):
        if next_kb is not None:
            for h in range(2):
                scores(next_kb, h, 1 - slot)
        for h in range(2):
            update(kb, h, slot, visible)

    for h in range(2):
        scores(i, h, 0)
    k_chunk = lax.broadcasted_iota(jnp.int32, (tk, tq), 0) // CHUNK
    q_chunk = lax.broadcasted_iota(jnp.int32, (tk, tq), 1) // CHUNK
    step(0, i, 0, k_chunk <= q_chunk)

    def body(p, c):
        step(2 * p + 1, 2 * p, 1)
        step(2 * p + 2, 2 * p + 1, 0)
        return c

    lax.fori_loop(0, (i - 1) // 2, body, 0)

    @pl.when(i % 2 == 1)
    def _():
        step(None, i - 1, 1)

    @pl.when((i % 2 == 0) & (i > 0))
    def _():
        step(i - 1, i - 2, 1)
        step(None, i - 1, 0)

    o_t = jnp.concatenate([acc_sc[h, :half] / acc_sc[h, half:half + 1] for h in range(2)], axis=0)
    o_ref[...] = o_t.T.astype(o_ref.dtype)


def _mla_prompt_attn(q, k, vt):
    s = k.shape[0]
    tq = ATTN_TQ
    assert ATTN_TQ == ATTN_TK and vt.shape == (s // ATTN_TK, LANES * N_PAIRS, ATTN_TK)
    return pl.pallas_call(
        _mla_prompt_attn_kernel,
        grid=(N_PAIRS, s // tq),
        in_specs=[pl.BlockSpec((tq, 2 * LANES), lambda j, i: (i, j)),
                  pl.BlockSpec((s, 2 * LANES), lambda j, i: (0, j)),
                  pl.BlockSpec((s // ATTN_TK, LANES, ATTN_TK), lambda j, i: (0, j, 0))],
        out_specs=pl.BlockSpec((tq, LANES), lambda j, i: (i, j)),
        out_shape=jax.ShapeDtypeStruct((s, LANES * N_PAIRS), BF16),
        scratch_shapes=[pltpu.VMEM((2, tq, 2 * LANES), BF16),
                        pltpu.VMEM((2, 2, ATTN_TK, tq), F32),
                        pltpu.VMEM((2, 2, 1, tq), F32),
                        pltpu.VMEM((2, 1, tq), F32),
                        pltpu.VMEM((2, LANES // 2 + ONES_ROWS, tq), F32)],
        compiler_params=_params("parallel", "arbitrary"),
        name="mla_prompt_attn",
    )(q, k, vt)


def _static_visible(q_pos, k_pos, left_chunks=None):
    qc = (q_pos // CHUNK)[:, None]
    kc = (k_pos // CHUNK)[None, :]
    vis = kc <= qc
    if left_chunks is not None:
        vis = vis & (kc >= qc - left_chunks) & (k_pos[None, :] >= 0)
    return vis


def _mla_sample_attn_kernel(q_ref, cc_ref, rc_ref, cn_ref, rn_ref, wukt_ref, wuv_ref, o_ref, *,
                            visible):
    nope_masks = _pair_query_masks(2 * LANES)
    lane = lax.broadcasted_iota(jnp.int32, (1, LANES), 1)
    low = lane < LANES // 2
    q_lat, q_rope = [], []
    for j in range(N_PAIRS):
        q = q_ref[0, :, 2 * LANES * j:2 * LANES * (j + 1)]
        for h in range(2):
            qm = jnp.where(nope_masks[h], q, jnp.zeros_like(q))
            q_lat.append(_dot(qm[:, :LANES], wukt_ref[LANES * j:LANES * (j + 1), :]).astype(BF16))
            q_rope.append(qm[:, LANES:])
    q_lat = jnp.concatenate(q_lat, axis=0)
    q_rope = jnp.concatenate(q_rope, axis=0)
    cc = cc_ref[0].astype(BF16)
    cn = cn_ref[0].astype(BF16)
    s_c = _dot_nt(q_lat, cc) + _dot_nt(q_rope, rc_ref[0])
    s_n = _dot_nt(q_lat, cn) + _dot_nt(q_rope, rn_ref[0].astype(BF16))
    if visible is not None:
        vis_c, vis_n = (jnp.asarray(np.tile(v, (MLA_HEADS, 1))) for v in visible)
        s_c = jnp.where(vis_c, s_c, NEG_INF)
        s_n = jnp.where(vis_n, s_n, NEG_INF)
    m = jnp.maximum(jnp.max(s_c, -1, keepdims=True), jnp.max(s_n, -1, keepdims=True))
    p_c = jnp.exp2(s_c - m)
    p_n = jnp.exp2(s_n - m)
    l = jnp.sum(p_c, -1, keepdims=True) + jnp.sum(p_n, -1, keepdims=True)
    o_lat = ((_dot(p_c.astype(BF16), cc) + _dot(p_n.astype(BF16), cn)) / l).astype(BF16)
    t = q_ref.shape[1]
    for j in range(N_PAIRS):
        w = wuv_ref[:, LANES * j:LANES * (j + 1)]
        o0 = _dot(o_lat[2 * j * t:(2 * j + 1) * t], w)
        o1 = _dot(o_lat[(2 * j + 1) * t:(2 * j + 2) * t], w)
        o_ref[0, :, LANES * j:LANES * (j + 1)] = jnp.where(low, o0, o1).astype(o_ref.dtype)


def _mla_sample_attn(q, cache_ckv, cache_kr128, new_ckv, new_kr128, wuk_t, wuv):
    b, t, _ = q.shape
    n_past = cache_ckv.shape[1]
    q_pos = PAST_LEN + np.arange(t)
    vis_c = _static_visible(q_pos, PAST_LEN - n_past + np.arange(n_past))
    vis_n = _static_visible(q_pos, q_pos)
    visible = None if (vis_c.all() and vis_n.all()) else (vis_c, vis_n)
    blk = lambda a: pl.BlockSpec((1,) + a.shape[1:], lambda i: (i, 0, 0))
    full = lambda a: pl.BlockSpec(a.shape, lambda i: (0, 0))
    return pl.pallas_call(
        functools.partial(_mla_sample_attn_kernel, visible=visible),
        grid=(b,),
        in_specs=[blk(q), blk(cache_ckv), blk(cache_kr128), blk(new_ckv), blk(new_kr128),
                  full(wuk_t), full(wuv)],
        out_specs=pl.BlockSpec((1, t, LANES * N_PAIRS), lambda i: (i, 0, 0)),
        out_shape=jax.ShapeDtypeStruct((b, t, LANES * N_PAIRS), BF16),
        compiler_params=_params("parallel"),
        name="mla_sample_attn",
    )(q, cache_ckv, cache_kr128, new_ckv, new_kr128, wuk_t, wuv)


def _out_ln_kernel(op_ref, os_ref, w_ref, *refs, prompt_tiles):
    *x_refs, g_ref, b_ref, y_ref = refs
    x = x_refs[0][...] if len(x_refs) == 1 else _stream_tile(*x_refs, prompt_tiles)
    y = DEEPNORM_ALPHA * x + _dot(_stream_tile(op_ref, os_ref, prompt_tiles), w_ref[...])
    y_ref[...] = _layer_norm(y, g_ref[...], b_ref[...])


def _out_ln(o_p, o_s, w_o, x, g, b):
    tm = TOKEN_TILE
    n_p = o_p.shape[0] // tm
    n = o_p.shape[0] + o_s.shape[0]
    row = pl.BlockSpec((tm, D_MODEL), lambda i: (i, 0))
    full = lambda a: pl.BlockSpec(a.shape, lambda i: (0, 0))
    xs = list(x) if isinstance(x, tuple) else [x]
    x_specs = _stream_specs(tm, D_MODEL, n_p) if isinstance(x, tuple) else [row]
    return pl.pallas_call(
        functools.partial(_out_ln_kernel, prompt_tiles=n_p),
        grid=(n // tm,),
        in_specs=_stream_specs(tm, D_MODEL, n_p) + [full(w_o)] + x_specs + [full(g), full(b)],
        out_specs=row,
        out_shape=jax.ShapeDtypeStruct((n, D_MODEL), F32),
        compiler_params=_params("parallel"),
        name="out_proj_ln",
    )(o_p, o_s, w_o, *xs, g, b)


def _band_qkv_kernel(x_ref, wqk_ref, wv_ref, wvt_ref, q_ref, k_ref, vt_ref, kf_ref, vf_ref, *,
                     first_keep_tile):
    xb = x_ref[...].astype(BF16)
    qk = _dot(xb, wqk_ref[...])
    scale = math.log2(math.e) / math.sqrt(BAND_HEAD_DIM)
    q_ref[...] = (qk[:, :D_MODEL] * scale).astype(BF16)
    k = qk[:, D_MODEL:]
    k_ref[...] = k.astype(BF16)
    vt = _dot_nt(wvt_ref[...], xb).astype(BF16)
    for t in range(vt_ref.shape[0]):
        vt_ref[t] = vt[:, LANES * t:LANES * (t + 1)]

    @pl.when(pl.program_id(0) >= first_keep_tile)
    def _():
        kf_ref[...] = k
        vf_ref[...] = _dot(xb, wv_ref[...])


def _band_qkv(x, w_qkv, n_keep):
    n = x.shape[0]
    tm = TOKEN_TILE
    first = (n - n_keep) // tm
    w = w_qkv.astype(BF16)
    wqk, wv = w[:, :2 * D_MODEL], w[:, 2 * D_MODEL:]
    row = pl.BlockSpec((tm, D_MODEL), lambda i: (i, 0))
    keep = pl.BlockSpec((tm, D_MODEL), lambda i: (jnp.maximum(i - first, 0), 0))
    full = lambda a: pl.BlockSpec(a.shape, lambda i: (0, 0))
    bf = jax.ShapeDtypeStruct((n, D_MODEL), BF16)
    kf = jax.ShapeDtypeStruct((n_keep, D_MODEL), F32)
    return pl.pallas_call(
        functools.partial(_band_qkv_kernel, first_keep_tile=first),
        grid=(n // tm,),
        in_specs=[row, full(wqk), full(wv), full(wv)],
        out_specs=[row, row, pl.BlockSpec((tm // LANES, D_MODEL, LANES), lambda i: (i, 0, 0)),
                   keep, keep],
        out_shape=[bf, bf, jax.ShapeDtypeStruct((n // LANES, D_MODEL, LANES), BF16), kf, kf],
        compiler_params=_params("arbitrary"),
        name="band_qkv",
    )(x, wqk, wv, wv.T)


def _band_prompt_attn_kernel(q_ref, k_ref, vt_ref, rel_ref, o_ref, bias_sc):
    i = pl.program_id(1)
    half = LANES // 2
    low = lax.broadcasted_iota(jnp.int32, (1, LANES), 1) < half
    n_win = BAND_WIN // LANES
    n_left = BAND_KEEP_MAX // LANES

    @pl.when(i == 0)
    def _():
        k_in = lax.broadcasted_iota(jnp.int32, (LANES, LANES), 0)
        q_in = lax.broadcasted_iota(jnp.int32, (LANES, LANES), 1)
        for t in range(n_left + 1):
            q_chunk = (BAND_QSUB * t + q_in) // CHUNK
            for kb in range(n_win):
                k_chunk = (LANES * kb + k_in) // CHUNK
                visible = (k_chunk <= q_chunk) & (k_chunk >= q_chunk - LEFT_CHUNKS)
                start = LANES * (t - kb + n_left)
                for h in range(2):
                    run = jnp.broadcast_to(rel_ref[h, :, start:start + 2 * LANES], (LANES, 2 * LANES))
                    tile = pltpu.roll(run, 0, 1, stride=1, stride_axis=0)[:, LANES:]
                    bias_sc[t, LANES * kb:LANES * (kb + 1), LANES * h:LANES * (h + 1)] = jnp.where(
                        visible, tile, NEG_INF)

    def body(g, c):
        work = []
        for u in range(BAND_GROUP):
            sb = g * BAND_GROUP + u
            blk = i * (BAND_QBLOCK // BAND_QSUB) + sb
            b0 = jnp.maximum(blk - n_left, 0)
            rows = pl.ds(pl.multiple_of(sb * BAND_QSUB, BAND_QSUB), BAND_QSUB)
            q = q_ref[rows, :]
            zero = jnp.zeros_like(q)
            qq = jnp.concatenate([jnp.where(low, q, zero), jnp.where(low, zero, q)], axis=0)
            kw = k_ref[pl.ds(pl.multiple_of(b0 * LANES, LANES), BAND_WIN), :]
            s = _dot_nt(kw, qq) + bias_sc[jnp.minimum(blk, n_left)]
            work.append((rows, b0, s))
        for rows, b0, s in work:
            m = jnp.max(s, 0, keepdims=True)
            p = jnp.exp2(s - m)
            l = jnp.sum(p, 0, keepdims=True)
            pb = p.astype(BF16)
            vt = jnp.concatenate([vt_ref[b0 + w] for w in range(n_win)], axis=1)
            o_t = jnp.concatenate(
                [_dot(vt[half * h:half * (h + 1)], pb[:, LANES * h:LANES * (h + 1)])
                 / l[:, LANES * h:LANES * (h + 1)] for h in range(2)], axis=0)
            o_ref[rows, :] = o_t.T.astype(o_ref.dtype)
        return c

    lax.fori_loop(0, BAND_QBLOCK // BAND_QSUB // BAND_GROUP, body, 0)


def _band_prompt_attn(q, k, vt, rel_table, s):
    n_left = BAND_KEEP_MAX // LANES
    offs = np.arange(LANES * (2 * n_left + 2)) - LANES * (n_left + 1)
    rel_run = (rel_table.astype(F32)[:, np.clip(offs, -REL_MAX, REL_MAX) + REL_MAX]
               * math.log2(math.e))[:, None, :]
    return pl.pallas_call(
        _band_prompt_attn_kernel,
        grid=(N_PAIRS, s // BAND_QBLOCK),
        in_specs=[pl.BlockSpec((BAND_QBLOCK, LANES), lambda j, i: (i, j)),
                  pl.BlockSpec((s, LANES), lambda j, i: (0, j)),
                  pl.BlockSpec((s // LANES, LANES, LANES), lambda j, i: (0, j, 0)),
                  pl.BlockSpec((2, 1, rel_run.shape[-1]), lambda j, i: (j, 0, 0))],
        out_specs=pl.BlockSpec((BAND_QBLOCK, LANES), lambda j, i: (i, j)),
        out_shape=jax.ShapeDtypeStruct((s, D_MODEL), BF16),
        scratch_shapes=[pltpu.VMEM((n_left + 1, BAND_WIN, 2 * LANES), F32)],
        compiler_params=_params("parallel", "arbitrary"),
        name="band_prompt_attn",
    )(q, k, vt, rel_run)


def _band_sample_attn_kernel(q_ref, kc_ref, vc_ref, kn_ref, vn_ref, bc_ref, bn_ref, o_ref):
    lane = lax.broadcasted_iota(jnp.int32, (1, LANES), 1)
    low = lane < LANES // 2
    for j in range(N_PAIRS):
        cols = slice(LANES * j, LANES * (j + 1))
        q = q_ref[0, :, cols]
        kc = kc_ref[0, :, cols].astype(BF16)
        vc = vc_ref[0, :, cols].astype(BF16)
        kn = kn_ref[0, :, cols].astype(BF16)
        vn = vn_ref[0, :, cols].astype(BF16)
        outs = []
        for h in range(2):
            qh = jnp.where(low if h == 0 else ~low, q, jnp.zeros_like(q))
            sc = _dot_nt(qh, kc) + bc_ref[2 * j + h]
            sn = _dot_nt(qh, kn) + bn_ref[2 * j + h]
            m = jnp.maximum(jnp.max(sc, -1, keepdims=True), jnp.max(sn, -1, keepdims=True))
            pc = jnp.exp2(sc - m)
            pn = jnp.exp2(sn - m)
            l = jnp.sum(pc, -1, keepdims=True) + jnp.sum(pn, -1, keepdims=True)
            outs.append((_dot(pc.astype(BF16), vc) + _dot(pn.astype(BF16), vn)) / l)
        o_ref[0, :, cols] = jnp.where(low, outs[0], outs[1]).astype(o_ref.dtype)


def _band_sample_attn(q, kc, vc, kn, vn, bias_c, bias_n):
    b, t, _ = q.shape
    blk = lambda a: pl.BlockSpec((1,) + a.shape[1:], lambda i: (i, 0, 0))
    full = lambda a: pl.BlockSpec(a.shape, lambda i: (0, 0, 0))
    return pl.pallas_call(
        _band_sample_attn_kernel,
        grid=(b,),
        in_specs=[blk(q), blk(kc), blk(vc), blk(kn), blk(vn), full(bias_c), full(bias_n)],
        out_specs=pl.BlockSpec((1, t, D_MODEL), lambda i: (i, 0, 0)),
        out_shape=jax.ShapeDtypeStruct((b, t, D_MODEL), BF16),
        compiler_params=_params("parallel"),
        name="band_sample_attn",
    )(q, kc, vc, kn, vn, bias_c, bias_n)


def _band_bias(rel_table, q_pos, k_pos):
    nq, nk = len(q_pos), len(k_pos)
    assert (np.diff(q_pos) == 1).all() and (np.diff(k_pos) == 1).all()
    span = nq + nk - 1
    diag = int(q_pos[0] - k_pos[0]) + nq - 1 - np.arange(span)
    u = rel_table.astype(F32)[:, np.clip(diag, -REL_MAX, REL_MAX) + REL_MAX] * math.log2(math.e)
    u = jnp.pad(u, ((0, 0), (0, 1)))
    skew = jnp.tile(u, (1, nq))[:, :nq * span].reshape(-1, nq, span)
    bias = skew[:, :, nq - 1:nq - 1 + nk]
    vis = _static_visible(q_pos, k_pos, LEFT_CHUNKS)
    return jnp.where(jnp.asarray(vis)[None], bias, NEG_INF)


def _route(logits_t, rb_ref):
    scores = jax.nn.sigmoid(logits_t)
    rows = [scores[e:e + 1, :] for e in range(N_EXPERTS)]
    biased = [rows[e] + rb_ref[e] for e in range(N_EXPERTS)]
    best = None
    sel = None
    for g in range(N_GROUPS):
        a, b, c, d = biased[EXPERTS_PER_GROUP * g:EXPERTS_PER_GROUP * (g + 1)]
        top2 = jnp.maximum(jnp.maximum(jnp.maximum(a + b, a + c), jnp.maximum(a + d, b + c)),
                           jnp.maximum(b + d, c + d))
        if g == 0:
            best, sel = top2, jnp.zeros(top2.shape, jnp.int32)
        else:
            better = top2 > best
            sel = jnp.where(better, g, sel)
            best = jnp.where(better, top2, best)
    cand = [jnp.where(sel == e // EXPERTS_PER_GROUP, biased[e], NEG_INF) for e in range(N_EXPERTS)]

    def argmax_first(vals):
        top = functools.reduce(jnp.maximum, vals)
        idx = jnp.full(top.shape, N_EXPERTS, jnp.int32)
        for e in reversed(range(N_EXPERTS)):
            idx = jnp.where(vals[e] == top, e, idx)
        return idx

    i1 = argmax_first(cand)
    i2 = argmax_first([jnp.where(i1 == e, -jnp.inf, cand[e]) for e in range(N_EXPERTS)])
    picked = [jnp.where((i1 == e) | (i2 == e), rows[e], 0.0) for e in range(N_EXPERTS)]
    total = functools.reduce(jnp.add, picked)
    return [p / total for p in picked]


def _moe_kernel(x_ref, rw_ref, rb_ref, wg_ref, wu_ref, wd_ref, g_ref, b_ref, *refs, prompt_tiles):
    *y_refs, xb_sc, comb_sc, acc_sc = refs
    grp = pl.program_id(1)

    @pl.when(grp == 0)
    def _():
        x = x_ref[...]
        xb = x.astype(BF16)
        xb_sc[...] = xb
        acc_sc[...] = DEEPNORM_ALPHA * x
        x_lo = (x - xb.astype(F32)).astype(BF16)
        rw_hi = rw_ref[0]
        logits_t = _dot_nt(rw_hi, xb) + (_dot_nt(rw_hi, x_lo) + _dot_nt(rw_ref[1], xb))
        comb_rows = _route(logits_t, rb_ref)
        comb = jnp.concatenate(comb_rows, axis=0).T
        for g in range(N_GROUPS):
            comb_sc[g] = comb[:, EXPERTS_PER_GROUP * g:EXPERTS_PER_GROUP * (g + 1)]

    xb = xb_sc[...]
    comb = comb_sc[grp]
    parts = []
    for e in range(EXPERTS_PER_GROUP):
        h = jax.nn.silu(_dot(xb, wg_ref[e])) * _dot(xb, wu_ref[e])
        parts.append((h * comb[:, e:e + 1]).astype(BF16))
    acc_sc[...] += _dot(jnp.concatenate(parts, axis=1), wd_ref[0])

    @pl.when(grp == N_GROUPS - 1)
    def _():
        y = _layer_norm(acc_sc[...], g_ref[...], b_ref[...])
        if len(y_refs) == 1:
            y_refs[0][...] = y
        else:
            is_prompt = pl.program_id(0) < prompt_tiles

            @pl.when(is_prompt)
            def _():
                y_refs[0][...] = y

            @pl.when(jnp.logical_not(is_prompt))
            def _():
                y_refs[1][...] = y


def _moe_ln(x, rw_t, rb, wg, wu, wd, layer, g, b, n_prompt=None):
    n = x.shape[0]
    tm = MOE_TILE
    row = pl.BlockSpec((tm, D_MODEL), lambda i, j: (i, 0))
    full = lambda a: pl.BlockSpec(a.shape, lambda i, j: (0, 0))
    grp = lambda a: pl.BlockSpec((None, a.shape[1] // N_GROUPS) + a.shape[2:],
                                 lambda i, j: (layer, j, 0, 0))
    if n_prompt is None:
        n_p = n // tm
        out_specs = row
        out_shape = jax.ShapeDtypeStruct((n, D_MODEL), F32)
        semantics = ("parallel", "arbitrary")
    else:
        n_p = n_prompt // tm
        out_specs = _stream_specs(tm, D_MODEL, n_p)
        out_shape = [jax.ShapeDtypeStruct((n_prompt, D_MODEL), F32),
                     jax.ShapeDtypeStruct((n - n_prompt, D_MODEL), F32)]
        semantics = ("arbitrary", "arbitrary")
    return pl.pallas_call(
        functools.partial(_moe_kernel, prompt_tiles=n_p),
        grid=(n // tm, N_GROUPS),
        in_specs=[row, pl.BlockSpec(rw_t.shape, lambda i, j: (0, 0, 0)),
                  pl.BlockSpec(memory_space=pltpu.SMEM),
                  grp(wg), grp(wu), grp(wd), full(g), full(b)],
        out_specs=out_specs,
        out_shape=out_shape,
        scratch_shapes=[pltpu.VMEM((tm, D_MODEL), BF16),
                        pltpu.VMEM((N_GROUPS, tm, EXPERTS_PER_GROUP), F32),
                        pltpu.VMEM((tm, D_MODEL), F32)],
        compiler_params=_params(*semantics),
        name="moe_ln",
    )(x, rw_t, rb, wg, wu, wd, g, b)


def _rope_tables():
    half = MLA_ROPE // 2
    inv = ROPE_THETA ** (-jnp.arange(half, dtype=F32) / half)
    pos = jnp.concatenate([jnp.arange(SEQ, dtype=jnp.int32),
                           jnp.tile(PAST_LEN + jnp.arange(DEC_SEQ, dtype=jnp.int32), DEC_BATCH)])
    ang = pos.astype(F32)[:, None] * inv[None, :]
    reps = LANES // half
    return jnp.tile(jnp.cos(ang), (1, reps)), jnp.tile(jnp.sin(ang), (1, reps))


def _rot_cols(w):
    half = MLA_ROPE // 2
    return jnp.concatenate([-w[..., half:], w[..., :half]], -1)


def _mla_weights(w_in, w_uq, w_uk, w_uv, w_o):
    o_kv = MLA_Q_LORA + MLA_KV_LORA
    w_r = w_in[:, o_kv:]
    zpad = jnp.zeros((D_MODEL, LANES - 2 * MLA_ROPE), F32)
    w_in_p = jnp.concatenate([w_in[:, :o_kv], w_r, w_r, zpad,
                              _rot_cols(w_r), _rot_cols(w_r), zpad], 1).astype(BF16)
    wq = w_uq.reshape(MLA_Q_LORA, N_PAIRS, 2, MLA_QK)
    nope = wq[..., :MLA_NOPE].reshape(MLA_Q_LORA, N_PAIRS, 2 * MLA_NOPE)
    ropew = wq[..., MLA_NOPE:]
    zq = jnp.zeros((MLA_Q_LORA, N_PAIRS, LANES - 2 * MLA_ROPE), F32)
    rope_cols = jnp.concatenate([ropew.reshape(MLA_Q_LORA, N_PAIRS, 2 * MLA_ROPE), zq], -1)
    rot_cols = jnp.concatenate([_rot_cols(ropew).reshape(MLA_Q_LORA, N_PAIRS, 2 * MLA_ROPE), zq], -1)
    wqa = jnp.concatenate([nope, rope_cols], -1).reshape(MLA_Q_LORA, 2 * LANES * N_PAIRS).astype(BF16)
    wqb = rot_cols.reshape(MLA_Q_LORA, LANES * N_PAIRS).astype(BF16)
    return w_in_p, wqa, wqb, w_uk.astype(BF16), w_uv.astype(BF16), w_o.astype(BF16)


def _group_experts(w_gate, w_up, w_down):
    wd = w_down.reshape(DEPTH, N_GROUPS, EXPERTS_PER_GROUP * D_EXPERT, D_MODEL).astype(BF16)
    return w_gate.astype(BF16), w_up.astype(BF16), wd


def kernel(x_prompt, x_sample, cache_mla_ckv, cache_mla_krope, cache_band_k, cache_band_v,
           mla_w_in, mla_q_norm, mla_kv_norm, mla_w_uq, mla_w_uk, mla_w_uv, mla_w_o,
           band_w_qkv, band_rel_bias, band_w_o,
           router_w, router_b, moe_w_gate, moe_w_up, moe_w_down,
           ln_mix_g, ln_mix_b, ln_ffn_g, ln_ffn_b):
    x = (x_prompt.reshape(SEQ, D_MODEL), x_sample.reshape(N_SAMPLE, D_MODEL))
    wg, wu, wd = _group_experts(moe_w_gate, moe_w_up, moe_w_down)
    cos, sin = _rope_tables()
    rw_f = router_w.T.astype(F32)
    rw_hi = rw_f.astype(BF16)
    rw_t = jnp.stack([rw_hi, (rw_f - rw_hi.astype(F32)).astype(BF16)], 0)
    rb = router_b.astype(F32)
    band_keep = cache_band_k.shape[2]
    row2 = lambda a: a.reshape(1, -1)

    ckv_p, kr_p, ckv_s, kr_s = [], [], [], []
    bk_p, bv_p, bk_s, bv_s = [], [], [], []
    for i in range(DEPTH):
        j = i // N_MIXERS
        if i % N_MIXERS == 0:
            w_in_p, wqa, wqb, wuk, wuv, w_o = _mla_weights(
                mla_w_in[j], mla_w_uq[j], mla_w_uk[j], mla_w_uv[j], mla_w_o[j])
            x_pair = x if isinstance(x, tuple) else (x[:SEQ], x[SEQ:])
            ckv, kr128, q = _mla_proj(*x_pair, w_in_p, row2(mla_q_norm[j]), row2(mla_kv_norm[j]),
                                      wqa, wqb, cos, sin)
            kr = kr128[:, :MLA_ROPE]
            ckv_p.append(ckv[:SEQ].reshape(1, SEQ, MLA_KV_LORA))
            kr_p.append(kr[:SEQ].reshape(1, SEQ, MLA_ROPE))
            ckv_s.append(ckv[SEQ:].reshape(DEC_BATCH, DEC_SEQ, MLA_KV_LORA))
            kr_s.append(kr[SEQ:].reshape(DEC_BATCH, DEC_SEQ, MLA_ROPE))
            k_p, vt_p = _kv_expand(ckv, kr128, wuk, wuv.T, SEQ)
            o_p = _mla_prompt_attn(q, k_p, vt_p)
            cache_kr = cache_mla_krope[j].astype(BF16)
            cache_kr128 = jnp.concatenate(
                [cache_kr, cache_kr, jnp.zeros(cache_kr.shape[:2] + (LANES - 2 * MLA_ROPE,), BF16)], -1)
            o_s = _mla_sample_attn(q[SEQ:].reshape(DEC_BATCH, DEC_SEQ, -1), cache_mla_ckv[j], cache_kr128,
                                   ckv_s[-1], kr128[SEQ:].reshape(DEC_BATCH, DEC_SEQ, LANES), wuk.T, wuv)
        else:
            n_keep = min(BAND_KEEP_MAX, SEQ) + N_SAMPLE
            x_all = jnp.concatenate(x, 0) if isinstance(x, tuple) else x
            q, k, vt, k_f, v_f = _band_qkv(x_all, band_w_qkv[j], n_keep)
            keep = n_keep - N_SAMPLE
            hd = (BAND_HEADS, BAND_HEAD_DIM)
            bk_p.append(k_f[:keep].reshape((1, keep) + hd))
            bv_p.append(v_f[:keep].reshape((1, keep) + hd))
            bk_s.append(k_f[keep:].reshape((DEC_BATCH, DEC_SEQ) + hd))
            bv_s.append(v_f[keep:].reshape((DEC_BATCH, DEC_SEQ) + hd))
            w_o = band_w_o[j].astype(BF16)
            o_p = _band_prompt_attn(q, k, vt, band_rel_bias[j], SEQ)
            q_pos = PAST_LEN + np.arange(DEC_SEQ)
            pos_c = PAST_LEN - band_keep + np.arange(band_keep)
            bias_c = _band_bias(band_rel_bias[j], q_pos, pos_c)
            bias_n = _band_bias(band_rel_bias[j], q_pos, q_pos)
            s3 = lambda a: a.reshape(DEC_BATCH, DEC_SEQ, D_MODEL)
            cache3 = lambda a: a.reshape(DEC_BATCH, band_keep, D_MODEL)
            o_s = _band_sample_attn(s3(q[SEQ:]), cache3(cache_band_k[j]), cache3(cache_band_v[j]),
                                    s3(k_f[keep:]), s3(v_f[keep:]), bias_c, bias_n)
        x = _out_ln(o_p, o_s.reshape(N_SAMPLE, D_MODEL), w_o, x, row2(ln_mix_g[i]), row2(ln_mix_b[i]))
        x = _moe_ln(x, rw_t, rb, wg, wu, wd, i, row2(ln_ffn_g[i]), row2(ln_ffn_b[i]),
                    n_prompt=SEQ if i == DEPTH - 1 else None)

    return (x[0].reshape(1, SEQ, D_MODEL), x[1].reshape(DEC_BATCH, DEC_SEQ, D_MODEL),
            jnp.stack(ckv_p, 0), jnp.stack(kr_p, 0), jnp.stack(ckv_s, 0), jnp.stack(kr_s, 0),
            jnp.stack(bk_p, 0), jnp.stack(bv_p, 0), jnp.stack(bk_s, 0), jnp.stack(bv_s, 0))
```

```python
import functools
import math

import numpy as np
import jax
import jax.numpy as jnp
from jax import lax
from jax.experimental import pallas as pl
from jax.experimental.pallas import tpu as pltpu

D_MODEL = 1024
SEQ = 16384
DEPTH = 2
DEC_BATCH = 32
DEC_SEQ = 32
PAST_LEN = 1024
CHUNK = 64
N_MIXERS = 2

MLA_HEADS = 16
MLA_Q_LORA = 384
MLA_KV_LORA = 256
MLA_NOPE = 64
MLA_ROPE = 32
MLA_QK = MLA_NOPE + MLA_ROPE
MLA_V = 64
ROPE_THETA = 10000.0

BAND_HEADS = 16
BAND_HEAD_DIM = D_MODEL // BAND_HEADS
LEFT_CHUNKS = 8
BAND_KEEP_MAX = LEFT_CHUNKS * CHUNK
REL_MAX = 128

N_EXPERTS = 16
N_GROUPS = 4
EXPERTS_PER_GROUP = N_EXPERTS // N_GROUPS
D_EXPERT = 256

DEEPNORM_ALPHA = (2.0 * DEPTH) ** 0.25
NORM_EPS = 1e-5
NEG_INF = -1e30

N_SAMPLE = DEC_BATCH * DEC_SEQ
N_TOKENS = SEQ + N_SAMPLE
N_PAIRS = MLA_HEADS // 2
LANES = 128
VMEM_LIMIT = 56 * 1024 * 1024

TOKEN_TILE = 512
MOE_TILE = 1024
ATTN_TQ = 512
ATTN_TK = 512
ONES_ROWS = 16
BAND_QSUB = 128
BAND_WIN = BAND_KEEP_MAX + BAND_QSUB
BAND_QBLOCK = 1024
BAND_GROUP = 8

BF16 = jnp.bfloat16
F32 = jnp.float32


def _params(*sem):
    return pltpu.CompilerParams(dimension_semantics=sem, vmem_limit_bytes=VMEM_LIMIT)


def _dot(a, b):
    return jnp.dot(a, b, preferred_element_type=F32)


def _dot_nt(a, b):
    return lax.dot_general(a, b, (((1,), (1,)), ((), ())), preferred_element_type=F32)


def _layer_norm(y, g, b):
    mu = jnp.mean(y, -1, keepdims=True)
    d = y - mu
    var = jnp.mean(d * d, -1, keepdims=True)
    return d * lax.rsqrt(var + NORM_EPS) * g + b


def _rms_norm(y, g):
    return y * lax.rsqrt(jnp.mean(y * y, -1, keepdims=True) + NORM_EPS) * g


def _stream_specs(tm, width, prompt_tiles):
    return [pl.BlockSpec((tm, width), lambda i, *_: (jnp.minimum(i, prompt_tiles - 1), 0)),
            pl.BlockSpec((tm, width), lambda i, *_: (jnp.maximum(i - prompt_tiles, 0), 0))]


def _stream_tile(p_ref, s_ref, prompt_tiles):
    return jnp.where(pl.program_id(0) < prompt_tiles, p_ref[...], s_ref[...])


def _mla_proj_kernel(xp_ref, xs_ref, w_in_ref, qn_ref, kvn_ref, wqa_ref, wqb_ref, cos_ref, sin_ref,
                     ckv_ref, kr_ref, q_ref, *, prompt_tiles):
    xb = _stream_tile(xp_ref, xs_ref, prompt_tiles).astype(BF16)
    a = _dot(xb, w_in_ref[...])
    cq = _rms_norm(a[:, :MLA_Q_LORA], qn_ref[...])
    o_kv = MLA_Q_LORA + MLA_KV_LORA
    ckv_ref[...] = _rms_norm(a[:, MLA_Q_LORA:o_kv], kvn_ref[...])
    cos = cos_ref[...]
    sin = sin_ref[...]
    kr_ref[...] = a[:, o_kv:o_kv + LANES] * cos + a[:, o_kv + LANES:] * sin
    cqb = cq.astype(BF16)
    qa = _dot(cqb, wqa_ref[...])
    qb = _dot(cqb, wqb_ref[...])
    scale = math.log2(math.e) / math.sqrt(MLA_QK)
    for j in range(N_PAIRS):
        c0 = 2 * LANES * j
        q_ref[:, c0:c0 + LANES] = (qa[:, c0:c0 + LANES] * scale).astype(BF16)
        rot = qa[:, c0 + LANES:c0 + 2 * LANES] * cos + qb[:, LANES * j:LANES * (j + 1)] * sin
        q_ref[:, c0 + LANES:c0 + 2 * LANES] = (rot * scale).astype(BF16)


def _mla_proj(x_p, x_s, w_in_p, qn, kvn, wqa, wqb, cos, sin):
    n = x_p.shape[0] + x_s.shape[0]
    tm = TOKEN_TILE
    n_p = x_p.shape[0] // tm
    row = lambda w: pl.BlockSpec((tm, w), lambda i: (i, 0))
    full = lambda a: pl.BlockSpec(a.shape, lambda i: (0, 0))
    return pl.pallas_call(
        functools.partial(_mla_proj_kernel, prompt_tiles=n_p),
        grid=(n // tm,),
        in_specs=_stream_specs(tm, D_MODEL, n_p) + [full(w_in_p), full(qn), full(kvn), full(wqa),
                                                    full(wqb), row(LANES), row(LANES)],
        out_specs=[row(MLA_KV_LORA), row(LANES), row(2 * LANES * N_PAIRS)],
        out_shape=[jax.ShapeDtypeStruct((n, MLA_KV_LORA), F32),
                   jax.ShapeDtypeStruct((n, LANES), F32),
                   jax.ShapeDtypeStruct((n, 2 * LANES * N_PAIRS), BF16)],
        compiler_params=_params("parallel"),
        name="mla_proj",
    )(x_p, x_s, w_in_p, qn, kvn, wqa, wqb, cos, sin)


def _kv_expand_kernel(ckv_ref, kr_ref, wuk_ref, wuvt_ref, k_ref, vt_ref):
    cb = ckv_ref[...].astype(BF16)
    kn = _dot(cb, wuk_ref[...])
    vt_ref[0] = _dot_nt(wuvt_ref[...], cb).astype(BF16)
    krb = kr_ref[...].astype(BF16)
    for j in range(N_PAIRS):
        c0 = 2 * LANES * j
        k_ref[:, c0:c0 + LANES] = kn[:, LANES * j:LANES * (j + 1)].astype(BF16)
        k_ref[:, c0 + LANES:c0 + 2 * LANES] = krb


def _kv_expand(ckv, kr128, wuk, wuv_t, n):
    tm = ATTN_TK
    row = lambda w: pl.BlockSpec((tm, w), lambda i: (i, 0))
    full = lambda a: pl.BlockSpec(a.shape, lambda i: (0, 0))
    nv = LANES * N_PAIRS
    return pl.pallas_call(
        _kv_expand_kernel,
        grid=(n // tm,),
        in_specs=[row(MLA_KV_LORA), row(LANES), full(wuk), full(wuv_t)],
        out_specs=[row(2 * LANES * N_PAIRS), pl.BlockSpec((1, nv, tm), lambda i: (i, 0, 0))],
        out_shape=[jax.ShapeDtypeStruct((n, 2 * LANES * N_PAIRS), BF16),
                   jax.ShapeDtypeStruct((n // tm, nv, tm), BF16)],
        compiler_params=_params("parallel"),
        name="mla_kv_expand",
    )(ckv, kr128, wuk, wuv_t)


def _pair_query_masks(width):
    lane = lax.broadcasted_iota(jnp.int32, (1, width), 1)
    half = LANES // 2
    r0 = LANES
    m0 = (lane < half) | ((lane >= r0) & (lane < r0 + MLA_ROPE))
    m1 = ((lane >= half) & (lane < LANES)) | ((lane >= r0 + MLA_ROPE) & (lane < r0 + 2 * MLA_ROPE))
    return m0, m1


def _mla_prompt_attn_kernel(q_ref, k_ref, vt_ref, o_ref, qh_sc, s_sc, bmax_sc, m_sc, acc_sc):
    i = pl.program_id(1)
    tq, tk = ATTN_TQ, ATTN_TK
    half = LANES // 2
    q = q_ref[...]
    for h, mask in enumerate(_pair_query_masks(2 * LANES)):
        qh_sc[h] = jnp.where(mask, q, jnp.zeros_like(q))
        m_sc[h] = jnp.full((1, tq), NEG_INF, F32)
        acc_sc[h] = jnp.zeros((half + ONES_ROWS, tq), F32)
    ones = jnp.ones((ONES_ROWS, tk), BF16)

    def scores(kb, h, slot):
        k = k_ref[pl.ds(pl.multiple_of(kb * tk, tk), tk), :]
        s = _dot_nt(k, qh_sc[h])
        s_sc[h, slot] = s
        bmax_sc[h, slot] = jnp.max(s, 0, keepdims=True)

    def update(kb, h, slot, visible=None):
        s = s_sc[h, slot]
        if visible is None:
            block_max = bmax_sc[h, slot]
        else:
            s = jnp.where(visible, s, NEG_INF)
            block_max = jnp.max(s, 0, keepdims=True)
        m_prev = m_sc[h]
        m_new = jnp.maximum(m_prev, block_max)
        alpha = jnp.exp2(m_prev - m_new)
        p = jnp.exp2(s - m_new).astype(BF16)
        vt = jnp.concatenate([vt_ref[kb, half * h:half * (h + 1), :], ones], axis=0)
        acc_sc[h] = alpha * acc_sc[h] + _dot(vt, p)
        m_sc[h] = m_new

    def step(next_kb, kb, slot, visible=None):
        if next_kb is not None:
            for h in range(2):
                scores(next_kb, h, 1 - slot)
        for h in range(2):
            update(kb, h, slot, visible)

    for h in range(2):
        scores(i, h, 0)
    k_chunk = lax.broadcasted_iota(jnp.int32, (tk, tq), 0) // CHUNK
    q_chunk = lax.broadcasted_iota(jnp.int32, (tk, tq), 1) // CHUNK
    step(0, i, 0, k_chunk <= q_chunk)

    def body(p, c):
        step(2 * p + 1, 2 * p, 1)
        step(2 * p + 2, 2 * p + 1, 0)
        return c

    lax.fori_loop(0, (i - 1) // 2, body, 0)

    @pl.when(i % 2 == 1)
    def _():
        step(None, i - 1, 1)

    @pl.when((i % 2 == 0) & (i > 0))
    def _():
        step(i - 1, i - 2, 1)
        step(None, i - 1, 0)

    o_t = jnp.concatenate([acc_sc[h, :half] / acc_sc[h, half:half + 1] for h in range(2)], axis=0)
    o_ref[...] = o_t.T.astype(o_ref.dtype)


def _mla_prompt_attn(q, k, vt):
    s = k.shape[0]
    tq = ATTN_TQ
    assert ATTN_TQ == ATTN_TK and vt.shape == (s // ATTN_TK, LANES * N_PAIRS, ATTN_TK)
    return pl.pallas_call(
        _mla_prompt_attn_kernel,
        grid=(N_PAIRS, s // tq),
        in_specs=[pl.BlockSpec((tq, 2 * LANES), lambda j, i: (i, j)),
                  pl.BlockSpec((s, 2 * LANES), lambda j, i: (0, j)),
                  pl.BlockSpec((s // ATTN_TK, LANES, ATTN_TK), lambda j, i: (0, j, 0))],
        out_specs=pl.BlockSpec((tq, LANES), lambda j, i: (i, j)),
        out_shape=jax.ShapeDtypeStruct((s, LANES * N_PAIRS), BF16),
        scratch_shapes=[pltpu.VMEM((2, tq, 2 * LANES), BF16),
                        pltpu.VMEM((2, 2, ATTN_TK, tq), F32),
                        pltpu.VMEM((2, 2, 1, tq), F32),
                        pltpu.VMEM((2, 1, tq), F32),
                        pltpu.VMEM((2, LANES // 2 + ONES_ROWS, tq), F32)],
        compiler_params=_params("parallel", "arbitrary"),
        name="mla_prompt_attn",
    )(q, k, vt)


def _static_visible(q_pos, k_pos, left_chunks=None):
    qc = (q_pos // CHUNK)[:, None]
    kc = (k_pos // CHUNK)[None, :]
    vis = kc <= qc
    if left_chunks is not None:
        vis = vis & (kc >= qc - left_chunks) & (k_pos[None, :] >= 0)
    return vis


def _mla_sample_attn_kernel(q_ref, cc_ref, rc_ref, cn_ref, rn_ref, wukt_ref, wuv_ref, o_ref, *,
                            visible):
    nope_masks = _pair_query_masks(2 * LANES)
    lane = lax.broadcasted_iota(jnp.int32, (1, LANES), 1)
    low = lane < LANES // 2
    q_lat, q_rope = [], []
    for j in range(N_PAIRS):
        q = q_ref[0, :, 2 * LANES * j:2 * LANES * (j + 1)]
        for h in range(2):
            qm = jnp.where(nope_masks[h], q, jnp.zeros_like(q))
            q_lat.append(_dot(qm[:, :LANES], wukt_ref[LANES * j:LANES * (j + 1), :]).astype(BF16))
            q_rope.append(qm[:, LANES:])
    q_lat = jnp.concatenate(q_lat, axis=0)
    q_rope = jnp.concatenate(q_rope, axis=0)
    cc = cc_ref[0].astype(BF16)
    cn = cn_ref[0].astype(BF16)
    s_c = _dot_nt(q_lat, cc) + _dot_nt(q_rope, rc_ref[0])
    s_n = _dot_nt(q_lat, cn) + _dot_nt(q_rope, rn_ref[0].astype(BF16))
    if visible is not None:
        vis_c, vis_n = (jnp.asarray(np.tile(v, (MLA_HEADS, 1))) for v in visible)
        s_c = jnp.where(vis_c, s_c, NEG_INF)
        s_n = jnp.where(vis_n, s_n, NEG_INF)
    m = jnp.maximum(jnp.max(s_c, -1, keepdims=True), jnp.max(s_n, -1, keepdims=True))
    p_c = jnp.exp2(s_c - m)
    p_n = jnp.exp2(s_n - m)
    l = jnp.sum(p_c, -1, keepdims=True) + jnp.sum(p_n, -1, keepdims=True)
    o_lat = ((_dot(p_c.astype(BF16), cc) + _dot(p_n.astype(BF16), cn)) / l).astype(BF16)
    t = q_ref.shape[1]
    for j in range(N_PAIRS):
        w = wuv_ref[:, LANES * j:LANES * (j + 1)]
        o0 = _dot(o_lat[2 * j * t:(2 * j + 1) * t], w)
        o1 = _dot(o_lat[(2 * j + 1) * t:(2 * j + 2) * t], w)
        o_ref[0, :, LANES * j:LANES * (j + 1)] = jnp.where(low, o0, o1).astype(o_ref.dtype)


def _mla_sample_attn(q, cache_ckv, cache_kr128, new_ckv, new_kr128, wuk_t, wuv):
    b, t, _ = q.shape
    n_past = cache_ckv.shape[1]
    q_pos = PAST_LEN + np.arange(t)
    vis_c = _static_visible(q_pos, PAST_LEN - n_past + np.arange(n_past))
    vis_n = _static_visible(q_pos, q_pos)
    visible = None if (vis_c.all() and vis_n.all()) else (vis_c, vis_n)
    blk = lambda a: pl.BlockSpec((1,) + a.shape[1:], lambda i: (i, 0, 0))
    full = lambda a: pl.BlockSpec(a.shape, lambda i: (0, 0))
    return pl.pallas_call(
        functools.partial(_mla_sample_attn_kernel, visible=visible),
        grid=(b,),
        in_specs=[blk(q), blk(cache_ckv), blk(cache_kr128), blk(new_ckv), blk(new_kr128),
                  full(wuk_t), full(wuv)],
        out_specs=pl.BlockSpec((1, t, LANES * N_PAIRS), lambda i: (i, 0, 0)),
        out_shape=jax.ShapeDtypeStruct((b, t, LANES * N_PAIRS), BF16),
        compiler_params=_params("parallel"),
        name="mla_sample_attn",
    )(q, cache_ckv, cache_kr128, new_ckv, new_kr128, wuk_t, wuv)


def _out_ln_kernel(op_ref, os_ref, w_ref, *refs, prompt_tiles):
    *x_refs, g_ref, b_ref, y_ref = refs
    x = x_refs[0][...] if len(x_refs) == 1 else _stream_tile(*x_refs, prompt_tiles)
    y = DEEPNORM_ALPHA * x + _dot(_stream_tile(op_ref, os_ref, prompt_tiles), w_ref[...])
    y_ref[...] = _layer_norm(y, g_ref[...], b_ref[...])


def _out_ln(o_p, o_s, w_o, x, g, b):
    tm = TOKEN_TILE
    n_p = o_p.shape[0] // tm
    n = o_p.shape[0] + o_s.shape[0]
    row = pl.BlockSpec((tm, D_MODEL), lambda i: (i, 0))
    full = lambda a: pl.BlockSpec(a.shape, lambda i: (0, 0))
    xs = list(x) if isinstance(x, tuple) else [x]
    x_specs = _stream_specs(tm, D_MODEL, n_p) if isinstance(x, tuple) else [row]
    return pl.pallas_call(
        functools.partial(_out_ln_kernel, prompt_tiles=n_p),
        grid=(n // tm,),
        in_specs=_stream_specs(tm, D_MODEL, n_p) + [full(w_o)] + x_specs + [full(g), full(b)],
        out_specs=row,
        out_shape=jax.ShapeDtypeStruct((n, D_MODEL), F32),
        compiler_params=_params("parallel"),
        name="out_proj_ln",
    )(o_p, o_s, w_o, *xs, g, b)


def _band_qkv_kernel(x_ref, wqk_ref, wv_ref, wvt_ref, q_ref, k_ref, vt_ref, kf_ref, vf_ref, *,
                     first_keep_tile):
    xb = x_ref[...].astype(BF16)
    qk = _dot(xb, wqk_ref[...])
    scale = math.log2(math.e) / math.sqrt(BAND_HEAD_DIM)
    q_ref[...] = (qk[:, :D_MODEL] * scale).astype(BF16)
    k = qk[:, D_MODEL:]
    k_ref[...] = k.astype(BF16)
    vt = _dot_nt(wvt_ref[...], xb).astype(BF16)
    for t in range(vt_ref.shape[0]):
        vt_ref[t] = vt[:, LANES * t:LANES * (t + 1)]

    @pl.when(pl.program_id(0) >= first_keep_tile)
    def _():
        kf_ref[...] = k
        vf_ref[...] = _dot(xb, wv_ref[...])


def _band_qkv(x, w_qkv, n_keep):
    n = x.shape[0]
    tm = TOKEN_TILE
    first = (n - n_keep) // tm
    w = w_qkv.astype(BF16)
    wqk, wv = w[:, :2 * D_MODEL], w[:, 2 * D_MODEL:]
    row = pl.BlockSpec((tm, D_MODEL), lambda i: (i, 0))
    keep = pl.BlockSpec((tm, D_MODEL), lambda i: (jnp.maximum(i - first, 0), 0))
    full = lambda a: pl.BlockSpec(a.shape, lambda i: (0, 0))
    bf = jax.ShapeDtypeStruct((n, D_MODEL), BF16)
    kf = jax.ShapeDtypeStruct((n_keep, D_MODEL), F32)
    return pl.pallas_call(
        functools.partial(_band_qkv_kernel, first_keep_tile=first),
        grid=(n // tm,),
        in_specs=[row, full(wqk), full(wv), full(wv)],
        out_specs=[row, row, pl.BlockSpec((tm // LANES, D_MODEL, LANES), lambda i: (i, 0, 0)),
                   keep, keep],
        out_shape=[bf, bf, jax.ShapeDtypeStruct((n // LANES, D_MODEL, LANES), BF16), kf, kf],
        compiler_params=_params("arbitrary"),
        name="band_qkv",
    )(x, wqk, wv, wv.T)


def _band_prompt_attn_kernel(q_ref, k_ref, vt_ref, rel_ref, o_ref, bias_sc):
    i = pl.program_id(1)
    half = LANES // 2
    low = lax.broadcasted_iota(jnp.int32, (1, LANES), 1) < half
    n_win = BAND_WIN // LANES
    n_left = BAND_KEEP_MAX // LANES

    @pl.when(i == 0)
    def _():
        k_in = lax.broadcasted_iota(jnp.int32, (LANES, LANES), 0)
        q_in = lax.broadcasted_iota(jnp.int32, (LANES, LANES), 1)
        for t in range(n_left + 1):
            q_chunk = (BAND_QSUB * t + q_in) // CHUNK
            for kb in range(n_win):
                k_chunk = (LANES * kb + k_in) // CHUNK
                visible = (k_chunk <= q_chunk) & (k_chunk >= q_chunk - LEFT_CHUNKS)
                start = LANES * (t - kb + n_left)
                for h in range(2):
                    run = jnp.broadcast_to(rel_ref[h, :, start:start + 2 * LANES], (LANES, 2 * LANES))
                    tile = pltpu.roll(run, 0, 1, stride=1, stride_axis=0)[:, LANES:]
                    bias_sc[t, LANES * kb:LANES * (kb + 1), LANES * h:LANES * (h + 1)] = jnp.where(
                        visible, tile, NEG_INF)

    def body(g, c):
        work = []
        for u in range(BAND_GROUP):
            sb = g * BAND_GROUP + u
            blk = i * (BAND_QBLOCK // BAND_QSUB) + sb
            b0 = jnp.maximum(blk - n_left, 0)
            rows = pl.ds(pl.multiple_of(sb * BAND_QSUB, BAND_QSUB), BAND_QSUB)
            q = q_ref[rows, :]
            zero = jnp.zeros_like(q)
            qq = jnp.concatenate([jnp.where(low, q, zero), jnp.where(low, zero, q)], axis=0)
            kw = k_ref[pl.ds(pl.multiple_of(b0 * LANES, LANES), BAND_WIN), :]
            s = _dot_nt(kw, qq) + bias_sc[jnp.minimum(blk, n_left)]
            work.append((rows, b0, s))
        for rows, b0, s in work:
            m = jnp.max(s, 0, keepdims=True)
            p = jnp.exp2(s - m)
            l = jnp.sum(p, 0, keepdims=True)
            pb = p.astype(BF16)
            vt = jnp.concatenate([vt_ref[b0 + w] for w in range(n_win)], axis=1)
            o_t = jnp.concatenate(
                [_dot(vt[half * h:half * (h + 1)], pb[:, LANES * h:LANES * (h + 1)])
                 / l[:, LANES * h:LANES * (h + 1)] for h in range(2)], axis=0)
            o_ref[rows, :] = o_t.T.astype(o_ref.dtype)
        return c

    lax.fori_loop(0, BAND_QBLOCK // BAND_QSUB // BAND_GROUP, body, 0)


def _band_prompt_attn(q, k, vt, rel_table, s):
    n_left = BAND_KEEP_MAX // LANES
    offs = np.arange(LANES * (2 * n_left + 2)) - LANES * (n_left + 1)
    rel_run = (rel_table.astype(F32)[:, np.clip(offs, -REL_MAX, REL_MAX) + REL_MAX]
               * math.log2(math.e))[:, None, :]
    return pl.pallas_call(
        _band_prompt_attn_kernel,
        grid=(N_PAIRS, s // BAND_QBLOCK),
        in_specs=[pl.BlockSpec((BAND_QBLOCK, LANES), lambda j, i: (i, j)),
                  pl.BlockSpec((s, LANES), lambda j, i: (0, j)),
                  pl.BlockSpec((s // LANES, LANES, LANES), lambda j, i: (0, j, 0)),
                  pl.BlockSpec((2, 1, rel_run.shape[-1]), lambda j, i: (j, 0, 0))],
        out_specs=pl.BlockSpec((BAND_QBLOCK, LANES), lambda j, i: (i, j)),
        out_shape=jax.ShapeDtypeStruct((s, D_MODEL), BF16),
        scratch_shapes=[pltpu.VMEM((n_left + 1, BAND_WIN, 2 * LANES), F32)],
        compiler_params=_params("parallel", "arbitrary"),
        name="band_prompt_attn",
    )(q, k, vt, rel_run)


def _band_sample_attn_kernel(q_ref, kc_ref, vc_ref, kn_ref, vn_ref, bc_ref, bn_ref, o_ref):
    lane = lax.broadcasted_iota(jnp.int32, (1, LANES), 1)
    low = lane < LANES // 2
    cols = [slice(LANES * j, LANES * (j + 1)) for j in range(N_PAIRS)]
    scores = []
    for j in range(N_PAIRS):
        q = q_ref[0, :, cols[j]]
        kc = kc_ref[0, :, cols[j]].astype(BF16)
        kn = kn_ref[0, :, cols[j]].astype(BF16)
        for h in range(2):
            qh = jnp.where(low if h == 0 else ~low, q, jnp.zeros_like(q))
            scores.append((_dot_nt(qh, kc) + bc_ref[2 * j + h], _dot_nt(qh, kn) + bn_ref[2 * j + h]))
    probs = []
    for sc, sn in scores:
        m = jnp.maximum(jnp.max(sc, -1, keepdims=True), jnp.max(sn, -1, keepdims=True))
        pc = jnp.exp2(sc - m)
        pn = jnp.exp2(sn - m)
        l = jnp.sum(pc, -1, keepdims=True) + jnp.sum(pn, -1, keepdims=True)
        probs.append((pc.astype(BF16), pn.astype(BF16), l))
    for j in range(N_PAIRS):
        vc = vc_ref[0, :, cols[j]].astype(BF16)
        vn = vn_ref[0, :, cols[j]].astype(BF16)
        outs = [(_dot(pc, vc) + _dot(pn, vn)) / l for pc, pn, l in probs[2 * j:2 * j + 2]]
        o_ref[0, :, cols[j]] = jnp.where(low, outs[0], outs[1]).astype(o_ref.dtype)


def _band_sample_attn(q, kc, vc, kn, vn, bias_c, bias_n):
    b, t, _ = q.shape
    blk = lambda a: pl.BlockSpec((1,) + a.shape[1:], lambda i: (i, 0, 0))
    full = lambda a: pl.BlockSpec(a.shape, lambda i: (0, 0, 0))
    return pl.pallas_call(
        _band_sample_attn_kernel,
        grid=(b,),
        in_specs=[blk(q), blk(kc), blk(vc), blk(kn), blk(vn), full(bias_c), full(bias_n)],
        out_specs=pl.BlockSpec((1, t, D_MODEL), lambda i: (i, 0, 0)),
        out_shape=jax.ShapeDtypeStruct((b, t, D_MODEL), BF16),
        compiler_params=_params("parallel"),
        name="band_sample_attn",
    )(q, kc, vc, kn, vn, bias_c, bias_n)


def _band_bias(rel_table, q_pos, k_pos):
    nq, nk = len(q_pos), len(k_pos)
    assert (np.diff(q_pos) == 1).all() and (np.diff(k_pos) == 1).all()
    span = nq + nk - 1
    diag = int(q_pos[0] - k_pos[0]) + nq - 1 - np.arange(span)
    u = rel_table.astype(F32)[:, np.clip(diag, -REL_MAX, REL_MAX) + REL_MAX] * math.log2(math.e)
    u = jnp.pad(u, ((0, 0), (0, 1)))
    skew = jnp.tile(u, (1, nq))[:, :nq * span].reshape(-1, nq, span)
    bias = skew[:, :, nq - 1:nq - 1 + nk]
    vis = _static_visible(q_pos, k_pos, LEFT_CHUNKS)
    return jnp.where(jnp.asarray(vis)[None], bias, NEG_INF)


def _route(logits_t, rb_ref):
    scores = jax.nn.sigmoid(logits_t)
    rows = [scores[e:e + 1, :] for e in range(N_EXPERTS)]
    biased = [rows[e] + rb_ref[e] for e in range(N_EXPERTS)]
    best = None
    sel = None
    for g in range(N_GROUPS):
        a, b, c, d = biased[EXPERTS_PER_GROUP * g:EXPERTS_PER_GROUP * (g + 1)]
        top2 = jnp.maximum(jnp.maximum(jnp.maximum(a + b, a + c), jnp.maximum(a + d, b + c)),
                           jnp.maximum(b + d, c + d))
        if g == 0:
            best, sel = top2, jnp.zeros(top2.shape, jnp.int32)
        else:
            better = top2 > best
            sel = jnp.where(better, g, sel)
            best = jnp.where(better, top2, best)
    cand = [jnp.where(sel == e // EXPERTS_PER_GROUP, biased[e], NEG_INF) for e in range(N_EXPERTS)]

    def argmax_first(vals):
        top = functools.reduce(jnp.maximum, vals)
        idx = jnp.full(top.shape, N_EXPERTS, jnp.int32)
        for e in reversed(range(N_EXPERTS)):
            idx = jnp.where(vals[e] == top, e, idx)
        return idx

    i1 = argmax_first(cand)
    i2 = argmax_first([jnp.where(i1 == e, -jnp.inf, cand[e]) for e in range(N_EXPERTS)])
    picked = [jnp.where((i1 == e) | (i2 == e), rows[e], 0.0) for e in range(N_EXPERTS)]
    total = functools.reduce(jnp.add, picked)
    return [p / total for p in picked]


def _moe_kernel(x_ref, rw_ref, rb_ref, wg_ref, wu_ref, wd_ref, g_ref, b_ref, *refs, prompt_tiles):
    *y_refs, xb_sc, comb_sc, acc_sc = refs
    grp = pl.program_id(1)

    def hidden(xb):
        return [jax.nn.silu(_dot(xb, wg_ref[e])) * _dot(xb, wu_ref[e])
                for e in range(EXPERTS_PER_GROUP)]

    def combine(hs, comb):
        parts = [(h * comb[:, e:e + 1]).astype(BF16) for e, h in enumerate(hs)]
        return _dot(jnp.concatenate(parts, axis=1), wd_ref[0])

    @pl.when(grp == 0)
    def _():
        x = x_ref[...]
        xb = x.astype(BF16)
        xb_sc[...] = xb
        hs = hidden(xb)
        x_lo = (x - xb.astype(F32)).astype(BF16)
        rw_hi = rw_ref[0]
        logits_t = _dot_nt(rw_hi, xb) + (_dot_nt(rw_hi, x_lo) + _dot_nt(rw_ref[1], xb))
        comb_rows = _route(logits_t, rb_ref)
        comb = jnp.concatenate(comb_rows, axis=0).T
        for g in range(N_GROUPS):
            comb_sc[g] = comb[:, EXPERTS_PER_GROUP * g:EXPERTS_PER_GROUP * (g + 1)]
        acc_sc[...] = DEEPNORM_ALPHA * x + combine(hs, comb[:, :EXPERTS_PER_GROUP])

    @pl.when(grp > 0)
    def _():
        acc_sc[...] += combine(hidden(xb_sc[...]), comb_sc[grp])

    @pl.when(grp == N_GROUPS - 1)
    def _():
        y = _layer_norm(acc_sc[...], g_ref[...], b_ref[...])
        if len(y_refs) == 1:
            y_refs[0][...] = y
        else:
            is_prompt = pl.program_id(0) < prompt_tiles

            @pl.when(is_prompt)
            def _():
                y_refs[0][...] = y

            @pl.when(jnp.logical_not(is_prompt))
            def _():
                y_refs[1][...] = y


def _moe_ln(x, rw_t, rb, wg, wu, wd, layer, g, b, n_prompt=None):
    n = x.shape[0]
    tm = MOE_TILE
    row = pl.BlockSpec((tm, D_MODEL), lambda i, j: (i, 0))
    full = lambda a: pl.BlockSpec(a.shape, lambda i, j: (0, 0))
    grp = lambda a: pl.BlockSpec((None, a.shape[1] // N_GROUPS) + a.shape[2:],
                                 lambda i, j: (layer, j, 0, 0))
    if n_prompt is None:
        n_p = n // tm
        out_specs = row
        out_shape = jax.ShapeDtypeStruct((n, D_MODEL), F32)
        semantics = ("parallel", "arbitrary")
    else:
        n_p = n_prompt // tm
        out_specs = _stream_specs(tm, D_MODEL, n_p)
        out_shape = [jax.ShapeDtypeStruct((n_prompt, D_MODEL), F32),
                     jax.ShapeDtypeStruct((n - n_prompt, D_MODEL), F32)]
        semantics = ("arbitrary", "arbitrary")
    return pl.pallas_call(
        functools.partial(_moe_kernel, prompt_tiles=n_p),
        grid=(n // tm, N_GROUPS),
        in_specs=[row, pl.BlockSpec(rw_t.shape, lambda i, j: (0, 0, 0)),
                  pl.BlockSpec(memory_space=pltpu.SMEM),
                  grp(wg), grp(wu), grp(wd), full(g), full(b)],
        out_specs=out_specs,
        out_shape=out_shape,
        scratch_shapes=[pltpu.VMEM((tm, D_MODEL), BF16),
                        pltpu.VMEM((N_GROUPS, tm, EXPERTS_PER_GROUP), F32),
                        pltpu.VMEM((tm, D_MODEL), F32)],
        compiler_params=_params(*semantics),
        name="moe_ln",
    )(x, rw_t, rb, wg, wu, wd, g, b)


def _rope_tables():
    half = MLA_ROPE // 2
    inv = ROPE_THETA ** (-jnp.arange(half, dtype=F32) / half)
    pos = jnp.concatenate([jnp.arange(SEQ, dtype=jnp.int32),
                           jnp.tile(PAST_LEN + jnp.arange(DEC_SEQ, dtype=jnp.int32), DEC_BATCH)])
    ang = pos.astype(F32)[:, None] * inv[None, :]
    reps = LANES // half
    return jnp.tile(jnp.cos(ang), (1, reps)), jnp.tile(jnp.sin(ang), (1, reps))


def _rot_cols(w):
    half = MLA_ROPE // 2
    return jnp.concatenate([-w[..., half:], w[..., :half]], -1)


def _mla_weights(w_in, w_uq, w_uk, w_uv, w_o):
    o_kv = MLA_Q_LORA + MLA_KV_LORA
    w_r = w_in[:, o_kv:]
    zpad = jnp.zeros((D_MODEL, LANES - 2 * MLA_ROPE), F32)
    w_in_p = jnp.concatenate([w_in[:, :o_kv], w_r, w_r, zpad,
                              _rot_cols(w_r), _rot_cols(w_r), zpad], 1).astype(BF16)
    wq = w_uq.reshape(MLA_Q_LORA, N_PAIRS, 2, MLA_QK)
    nope = wq[..., :MLA_NOPE].reshape(MLA_Q_LORA, N_PAIRS, 2 * MLA_NOPE)
    ropew = wq[..., MLA_NOPE:]
    zq = jnp.zeros((MLA_Q_LORA, N_PAIRS, LANES - 2 * MLA_ROPE), F32)
    rope_cols = jnp.concatenate([ropew.reshape(MLA_Q_LORA, N_PAIRS, 2 * MLA_ROPE), zq], -1)
    rot_cols = jnp.concatenate([_rot_cols(ropew).reshape(MLA_Q_LORA, N_PAIRS, 2 * MLA_ROPE), zq], -1)
    wqa = jnp.concatenate([nope, rope_cols], -1).reshape(MLA_Q_LORA, 2 * LANES * N_PAIRS).astype(BF16)
    wqb = rot_cols.reshape(MLA_Q_LORA, LANES * N_PAIRS).astype(BF16)
    return w_in_p, wqa, wqb, w_uk.astype(BF16), w_uv.astype(BF16), w_o.astype(BF16)


def _group_experts(w_gate, w_up, w_down):
    wd = w_down.reshape(DEPTH, N_GROUPS, EXPERTS_PER_GROUP * D_EXPERT, D_MODEL).astype(BF16)
    return w_gate.astype(BF16), w_up.astype(BF16), wd


def kernel(x_prompt, x_sample, cache_mla_ckv, cache_mla_krope, cache_band_k, cache_band_v,
           mla_w_in, mla_q_norm, mla_kv_norm, mla_w_uq, mla_w_uk, mla_w_uv, mla_w_o,
           band_w_qkv, band_rel_bias, band_w_o,
           router_w, router_b, moe_w_gate, moe_w_up, moe_w_down,
           ln_mix_g, ln_mix_b, ln_ffn_g, ln_ffn_b):
    x = (x_prompt.reshape(SEQ, D_MODEL), x_sample.reshape(N_SAMPLE, D_MODEL))
    wg, wu, wd = _group_experts(moe_w_gate, moe_w_up, moe_w_down)
    cos, sin = _rope_tables()
    rw_f = router_w.T.astype(F32)
    rw_hi = rw_f.astype(BF16)
    rw_t = jnp.stack([rw_hi, (rw_f - rw_hi.astype(F32)).astype(BF16)], 0)
    rb = router_b.astype(F32)
    band_keep = cache_band_k.shape[2]
    row2 = lambda a: a.reshape(1, -1)

    ckv_p, kr_p, ckv_s, kr_s = [], [], [], []
    bk_p, bv_p, bk_s, bv_s = [], [], [], []
    for i in range(DEPTH):
        j = i // N_MIXERS
        if i % N_MIXERS == 0:
            w_in_p, wqa, wqb, wuk, wuv, w_o = _mla_weights(
                mla_w_in[j], mla_w_uq[j], mla_w_uk[j], mla_w_uv[j], mla_w_o[j])
            x_pair = x if isinstance(x, tuple) else (x[:SEQ], x[SEQ:])
            ckv, kr128, q = _mla_proj(*x_pair, w_in_p, row2(mla_q_norm[j]), row2(mla_kv_norm[j]),
                                      wqa, wqb, cos, sin)
            kr = kr128[:, :MLA_ROPE]
            ckv_p.append(ckv[:SEQ].reshape(1, SEQ, MLA_KV_LORA))
            kr_p.append(kr[:SEQ].reshape(1, SEQ, MLA_ROPE))
            ckv_s.append(ckv[SEQ:].reshape(DEC_BATCH, DEC_SEQ, MLA_KV_LORA))
            kr_s.append(kr[SEQ:].reshape(DEC_BATCH, DEC_SEQ, MLA_ROPE))
            k_p, vt_p = _kv_expand(ckv, kr128, wuk, wuv.T, SEQ)
            o_p = _mla_prompt_attn(q, k_p, vt_p)
            cache_kr = cache_mla_krope[j].astype(BF16)
            cache_kr128 = jnp.concatenate(
                [cache_kr, cache_kr, jnp.zeros(cache_kr.shape[:2] + (LANES - 2 * MLA_ROPE,), BF16)], -1)
            o_s = _mla_sample_attn(q[SEQ:].reshape(DEC_BATCH, DEC_SEQ, -1), cache_mla_ckv[j], cache_kr128,
                                   ckv_s[-1], kr128[SEQ:].reshape(DEC_BATCH, DEC_SEQ, LANES), wuk.T, wuv)
        else:
            n_keep = min(BAND_KEEP_MAX, SEQ) + N_SAMPLE
            x_all = jnp.concatenate(x, 0) if isinstance(x, tuple) else x
            q, k, vt, k_f, v_f = _band_qkv(x_all, band_w_qkv[j], n_keep)
            keep = n_keep - N_SAMPLE
            hd = (BAND_HEADS, BAND_HEAD_DIM)
            bk_p.append(k_f[:keep].reshape((1, keep) + hd))
            bv_p.append(v_f[:keep].reshape((1, keep) + hd))
            bk_s.append(k_f[keep:].reshape((DEC_BATCH, DEC_SEQ) + hd))
            bv_s.append(v_f[keep:].reshape((DEC_BATCH, DEC_SEQ) + hd))
            w_o = band_w_o[j].astype(BF16)
            o_p = _band_prompt_attn(q, k, vt, band_rel_bias[j], SEQ)
            q_pos = PAST_LEN + np.arange(DEC_SEQ)
            pos_c = PAST_LEN - band_keep + np.arange(band_keep)
            bias_c = _band_bias(band_rel_bias[j], q_pos, pos_c)
            bias_n = _band_bias(band_rel_bias[j], q_pos, q_pos)
            s3 = lambda a: a.reshape(DEC_BATCH, DEC_SEQ, D_MODEL)
            cache3 = lambda a: a.reshape(DEC_BATCH, band_keep, D_MODEL)
            o_s = _band_sample_attn(s3(q[SEQ:]), cache3(cache_band_k[j]), cache3(cache_band_v[j]),
                                    s3(k_f[keep:]), s3(v_f[keep:]), bias_c, bias_n)
        x = _out_ln(o_p, o_s.reshape(N_SAMPLE, D_MODEL), w_o, x, row2(ln_mix_g[i]), row2(ln_mix_b[i]))
        x = _moe_ln(x, rw_t, rb, wg, wu, wd, i, row2(ln_ffn_g[i]), row2(ln_ffn_b[i]),
                    n_prompt=SEQ if i == DEPTH - 1 else None)

    return (x[0].reshape(1, SEQ, D_MODEL), x[1].reshape(DEC_BATCH, DEC_SEQ, D_MODEL),
            jnp.stack(ckv_p, 0), jnp.stack(kr_p, 0), jnp.stack(ckv_s, 0), jnp.stack(kr_s, 0),
            jnp.stack(bk_p, 0), jnp.stack(bv_p, 0), jnp.stack(bk_s, 0), jnp.stack(bv_s, 0))
```

```python
import functools
import math

import numpy as np
import jax
import jax.numpy as jnp
from jax import lax
from jax.experimental import pallas as pl
from jax.experimental.pallas import tpu as pltpu

D_MODEL = 1024
SEQ = 16384
DEPTH = 2
DEC_BATCH = 32
DEC_SEQ = 32
PAST_LEN = 1024
CHUNK = 64
N_MIXERS = 2

MLA_HEADS = 16
MLA_Q_LORA = 384
MLA_KV_LORA = 256
MLA_NOPE = 64
MLA_ROPE = 32
MLA_QK = MLA_NOPE + MLA_ROPE
MLA_V = 64
ROPE_THETA = 10000.0

BAND_HEADS = 16
BAND_HEAD_DIM = D_MODEL // BAND_HEADS
LEFT_CHUNKS = 8
BAND_KEEP_MAX = LEFT_CHUNKS * CHUNK
REL_MAX = 128

N_EXPERTS = 16
N_GROUPS = 4
EXPERTS_PER_GROUP = N_EXPERTS // N_GROUPS
D_EXPERT = 256

DEEPNORM_ALPHA = (2.0 * DEPTH) ** 0.25
NORM_EPS = 1e-5
NEG_INF = -1e30

N_SAMPLE = DEC_BATCH * DEC_SEQ
N_TOKENS = SEQ + N_SAMPLE
N_PAIRS = MLA_HEADS // 2
LANES = 128
VMEM_LIMIT = 56 * 1024 * 1024

TOKEN_TILE = 512
MOE_TILE = 1024
ATTN_TQ = 1024
ATTN_TK = 1024
ONES_ROWS = 16
BAND_QSUB = 128
BAND_WIN = BAND_KEEP_MAX + BAND_QSUB
BAND_QBLOCK = 1024
BAND_GROUP = 8

BF16 = jnp.bfloat16
F32 = jnp.float32


def _params(*sem):
    return pltpu.CompilerParams(dimension_semantics=sem, vmem_limit_bytes=VMEM_LIMIT)


def _dot(a, b):
    return jnp.dot(a, b, preferred_element_type=F32)


def _dot_nt(a, b):
    return lax.dot_general(a, b, (((1,), (1,)), ((), ())), preferred_element_type=F32)


def _layer_norm(y, g, b):
    mu = jnp.mean(y, -1, keepdims=True)
    d = y - mu
    var = jnp.mean(d * d, -1, keepdims=True)
    return d * lax.rsqrt(var + NORM_EPS) * g + b


def _rms_norm(y, g):
    return y * lax.rsqrt(jnp.mean(y * y, -1, keepdims=True) + NORM_EPS) * g


def _stream_specs(tm, width, prompt_tiles):
    return [pl.BlockSpec((tm, width), lambda i, *_: (jnp.minimum(i, prompt_tiles - 1), 0)),
            pl.BlockSpec((tm, width), lambda i, *_: (jnp.maximum(i - prompt_tiles, 0), 0))]


def _stream_tile(p_ref, s_ref, prompt_tiles):
    return jnp.where(pl.program_id(0) < prompt_tiles, p_ref[...], s_ref[...])


def _mla_proj_kernel(xp_ref, xs_ref, w_in_ref, qn_ref, kvn_ref, wqa_ref, wqb_ref, cos_ref, sin_ref,
                     ckv_ref, kr_ref, q_ref, *, prompt_tiles):
    xb = _stream_tile(xp_ref, xs_ref, prompt_tiles).astype(BF16)
    a = _dot(xb, w_in_ref[...])
    cq = _rms_norm(a[:, :MLA_Q_LORA], qn_ref[...])
    o_kv = MLA_Q_LORA + MLA_KV_LORA
    ckv_ref[...] = _rms_norm(a[:, MLA_Q_LORA:o_kv], kvn_ref[...])
    cos = cos_ref[...]
    sin = sin_ref[...]
    kr_ref[...] = a[:, o_kv:o_kv + LANES] * cos + a[:, o_kv + LANES:] * sin
    cqb = cq.astype(BF16)
    qa = _dot(cqb, wqa_ref[...])
    qb = _dot(cqb, wqb_ref[...])
    scale = math.log2(math.e) / math.sqrt(MLA_QK)
    for j in range(N_PAIRS):
        c0 = 2 * LANES * j
        q_ref[:, c0:c0 + LANES] = (qa[:, c0:c0 + LANES] * scale).astype(BF16)
        rot = qa[:, c0 + LANES:c0 + 2 * LANES] * cos + qb[:, LANES * j:LANES * (j + 1)] * sin
        q_ref[:, c0 + LANES:c0 + 2 * LANES] = (rot * scale).astype(BF16)


def _mla_proj(x_p, x_s, w_in_p, qn, kvn, wqa, wqb, cos, sin):
    n = x_p.shape[0] + x_s.shape[0]
    tm = TOKEN_TILE
    n_p = x_p.shape[0] // tm
    row = lambda w: pl.BlockSpec((tm, w), lambda i: (i, 0))
    full = lambda a: pl.BlockSpec(a.shape, lambda i: (0, 0))
    return pl.pallas_call(
        functools.partial(_mla_proj_kernel, prompt_tiles=n_p),
        grid=(n // tm,),
        in_specs=_stream_specs(tm, D_MODEL, n_p) + [full(w_in_p), full(qn), full(kvn), full(wqa),
                                                    full(wqb), row(LANES), row(LANES)],
        out_specs=[row(MLA_KV_LORA), row(LANES), row(2 * LANES * N_PAIRS)],
        out_shape=[jax.ShapeDtypeStruct((n, MLA_KV_LORA), F32),
                   jax.ShapeDtypeStruct((n, LANES), F32),
                   jax.ShapeDtypeStruct((n, 2 * LANES * N_PAIRS), BF16)],
        compiler_params=_params("parallel"),
        name="mla_proj",
    )(x_p, x_s, w_in_p, qn, kvn, wqa, wqb, cos, sin)


def _kv_expand_kernel(ckv_ref, kr_ref, wuk_ref, wuvt_ref, k_ref, vt_ref):
    cb = ckv_ref[...].astype(BF16)
    kn = _dot(cb, wuk_ref[...])
    vt_ref[0] = _dot_nt(wuvt_ref[...], cb).astype(BF16)
    krb = kr_ref[...].astype(BF16)
    for j in range(N_PAIRS):
        c0 = 2 * LANES * j
        k_ref[:, c0:c0 + LANES] = kn[:, LANES * j:LANES * (j + 1)].astype(BF16)
        k_ref[:, c0 + LANES:c0 + 2 * LANES] = krb


def _kv_expand(ckv, kr128, wuk, wuv_t, n):
    tm = ATTN_TK
    row = lambda w: pl.BlockSpec((tm, w), lambda i: (i, 0))
    full = lambda a: pl.BlockSpec(a.shape, lambda i: (0, 0))
    nv = LANES * N_PAIRS
    return pl.pallas_call(
        _kv_expand_kernel,
        grid=(n // tm,),
        in_specs=[row(MLA_KV_LORA), row(LANES), full(wuk), full(wuv_t)],
        out_specs=[row(2 * LANES * N_PAIRS), pl.BlockSpec((1, nv, tm), lambda i: (i, 0, 0))],
        out_shape=[jax.ShapeDtypeStruct((n, 2 * LANES * N_PAIRS), BF16),
                   jax.ShapeDtypeStruct((n // tm, nv, tm), BF16)],
        compiler_params=_params("parallel"),
        name="mla_kv_expand",
    )(ckv, kr128, wuk, wuv_t)


def _pair_query_masks(width):
    lane = lax.broadcasted_iota(jnp.int32, (1, width), 1)
    half = LANES // 2
    r0 = LANES
    m0 = (lane < half) | ((lane >= r0) & (lane < r0 + MLA_ROPE))
    m1 = ((lane >= half) & (lane < LANES)) | ((lane >= r0 + MLA_ROPE) & (lane < r0 + 2 * MLA_ROPE))
    return m0, m1


def _mla_prompt_attn_kernel(q_ref, k_ref, vt_ref, o_ref, qh_sc, s_sc, bmax_sc, m_sc, acc_sc):
    i = pl.program_id(1)
    tq, tk = ATTN_TQ, ATTN_TK
    half = LANES // 2
    q = q_ref[...]
    for h, mask in enumerate(_pair_query_masks(2 * LANES)):
        qh_sc[h] = jnp.where(mask, q, jnp.zeros_like(q))
        m_sc[h] = jnp.full((1, tq), NEG_INF, F32)
        acc_sc[h] = jnp.zeros((half + ONES_ROWS, tq), F32)
    ones = jnp.ones((ONES_ROWS, tk), BF16)

    def scores(kb, h, slot):
        k = k_ref[pl.ds(pl.multiple_of(kb * tk, tk), tk), :]
        s = _dot_nt(k, qh_sc[h])
        s_sc[h, slot] = s
        bmax_sc[h, slot] = jnp.max(s, 0, keepdims=True)

    def update(kb, h, slot, visible=None):
        s = s_sc[h, slot]
        if visible is None:
            block_max = bmax_sc[h, slot]
        else:
            s = jnp.where(visible, s, NEG_INF)
            block_max = jnp.max(s, 0, keepdims=True)
        m_prev = m_sc[h]
        m_new = jnp.maximum(m_prev, block_max)
        alpha = jnp.exp2(m_prev - m_new)
        p = jnp.exp2(s - m_new).astype(BF16)
        vt = jnp.concatenate([vt_ref[kb, half * h:half * (h + 1), :], ones], axis=0)
        acc_sc[h] = alpha * acc_sc[h] + _dot(vt, p)
        m_sc[h] = m_new

    def step(next_kb, kb, slot, visible=None):
        if next_kb is not None:
            for h in range(2):
                scores(next_kb, h, 1 - slot)
        for h in range(2):
            update(kb, h, slot, visible)

    for h in range(2):
        scores(i, h, 0)
    k_chunk = lax.broadcasted_iota(jnp.int32, (tk, tq), 0) // CHUNK
    q_chunk = lax.broadcasted_iota(jnp.int32, (tk, tq), 1) // CHUNK
    step(0, i, 0, k_chunk <= q_chunk)

    def body(p, c):
        step(2 * p + 1, 2 * p, 1)
        step(2 * p + 2, 2 * p + 1, 0)
        return c

    lax.fori_loop(0, (i - 1) // 2, body, 0)

    @pl.when(i % 2 == 1)
    def _():
        step(None, i - 1, 1)

    @pl.when((i % 2 == 0) & (i > 0))
    def _():
        step(i - 1, i - 2, 1)
        step(None, i - 1, 0)

    o_t = jnp.concatenate([acc_sc[h, :half] / acc_sc[h, half:half + 1] for h in range(2)], axis=0)
    o_ref[...] = o_t.T.astype(o_ref.dtype)


def _mla_prompt_attn(q, k, vt):
    s = k.shape[0]
    tq = ATTN_TQ
    assert ATTN_TQ == ATTN_TK and vt.shape == (s // ATTN_TK, LANES * N_PAIRS, ATTN_TK)
    return pl.pallas_call(
        _mla_prompt_attn_kernel,
        grid=(N_PAIRS, s // tq),
        in_specs=[pl.BlockSpec((tq, 2 * LANES), lambda j, i: (i, j)),
                  pl.BlockSpec((s, 2 * LANES), lambda j, i: (0, j)),
                  pl.BlockSpec((s // ATTN_TK, LANES, ATTN_TK), lambda j, i: (0, j, 0))],
        out_specs=pl.BlockSpec((tq, LANES), lambda j, i: (i, j)),
        out_shape=jax.ShapeDtypeStruct((s, LANES * N_PAIRS), BF16),
        scratch_shapes=[pltpu.VMEM((2, tq, 2 * LANES), BF16),
                        pltpu.VMEM((2, 2, ATTN_TK, tq), F32),
                        pltpu.VMEM((2, 2, 1, tq), F32),
                        pltpu.VMEM((2, 1, tq), F32),
                        pltpu.VMEM((2, LANES // 2 + ONES_ROWS, tq), F32)],
        compiler_params=_params("parallel", "arbitrary"),
        name="mla_prompt_attn",
    )(q, k, vt)


def _static_visible(q_pos, k_pos, left_chunks=None):
    qc = (q_pos // CHUNK)[:, None]
    kc = (k_pos // CHUNK)[None, :]
    vis = kc <= qc
    if left_chunks is not None:
        vis = vis & (kc >= qc - left_chunks) & (k_pos[None, :] >= 0)
    return vis


def _mla_sample_attn_kernel(q_ref, cc_ref, rc_ref, cn_ref, rn_ref, wukt_ref, wuv_ref, o_ref, *,
                            visible):
    nope_masks = _pair_query_masks(2 * LANES)
    lane = lax.broadcasted_iota(jnp.int32, (1, LANES), 1)
    low = lane < LANES // 2
    q_lat, q_rope = [], []
    for j in range(N_PAIRS):
        q = q_ref[0, :, 2 * LANES * j:2 * LANES * (j + 1)]
        for h in range(2):
            qm = jnp.where(nope_masks[h], q, jnp.zeros_like(q))
            q_lat.append(_dot(qm[:, :LANES], wukt_ref[LANES * j:LANES * (j + 1), :]).astype(BF16))
            q_rope.append(qm[:, LANES:])
    q_lat = jnp.concatenate(q_lat, axis=0)
    q_rope = jnp.concatenate(q_rope, axis=0)
    cc = cc_ref[0].astype(BF16)
    cn = cn_ref[0].astype(BF16)
    s_c = _dot_nt(q_lat, cc) + _dot_nt(q_rope, rc_ref[0])
    s_n = _dot_nt(q_lat, cn) + _dot_nt(q_rope, rn_ref[0].astype(BF16))
    if visible is not None:
        vis_c, vis_n = (jnp.asarray(np.tile(v, (MLA_HEADS, 1))) for v in visible)
        s_c = jnp.where(vis_c, s_c, NEG_INF)
        s_n = jnp.where(vis_n, s_n, NEG_INF)
    m = jnp.maximum(jnp.max(s_c, -1, keepdims=True), jnp.max(s_n, -1, keepdims=True))
    p_c = jnp.exp2(s_c - m)
    p_n = jnp.exp2(s_n - m)
    l = jnp.sum(p_c, -1, keepdims=True) + jnp.sum(p_n, -1, keepdims=True)
    o_lat = ((_dot(p_c.astype(BF16), cc) + _dot(p_n.astype(BF16), cn)) / l).astype(BF16)
    t = q_ref.shape[1]
    for j in range(N_PAIRS):
        w = wuv_ref[:, LANES * j:LANES * (j + 1)]
        o0 = _dot(o_lat[2 * j * t:(2 * j + 1) * t], w)
        o1 = _dot(o_lat[(2 * j + 1) * t:(2 * j + 2) * t], w)
        o_ref[0, :, LANES * j:LANES * (j + 1)] = jnp.where(low, o0, o1).astype(o_ref.dtype)


def _mla_sample_attn(q, cache_ckv, cache_kr128, new_ckv, new_kr128, wuk_t, wuv):
    b, t, _ = q.shape
    n_past = cache_ckv.shape[1]
    q_pos = PAST_LEN + np.arange(t)
    vis_c = _static_visible(q_pos, PAST_LEN - n_past + np.arange(n_past))
    vis_n = _static_visible(q_pos, q_pos)
    visible = None if (vis_c.all() and vis_n.all()) else (vis_c, vis_n)
    blk = lambda a: pl.BlockSpec((1,) + a.shape[1:], lambda i: (i, 0, 0))
    full = lambda a: pl.BlockSpec(a.shape, lambda i: (0, 0))
    return pl.pallas_call(
        functools.partial(_mla_sample_attn_kernel, visible=visible),
        grid=(b,),
        in_specs=[blk(q), blk(cache_ckv), blk(cache_kr128), blk(new_ckv), blk(new_kr128),
                  full(wuk_t), full(wuv)],
        out_specs=pl.BlockSpec((1, t, LANES * N_PAIRS), lambda i: (i, 0, 0)),
        out_shape=jax.ShapeDtypeStruct((b, t, LANES * N_PAIRS), BF16),
        compiler_params=_params("parallel"),
        name="mla_sample_attn",
    )(q, cache_ckv, cache_kr128, new_ckv, new_kr128, wuk_t, wuv)


def _out_ln_kernel(op_ref, os_ref, w_ref, *refs, prompt_tiles):
    *x_refs, g_ref, b_ref, y_ref = refs
    x = x_refs[0][...] if len(x_refs) == 1 else _stream_tile(*x_refs, prompt_tiles)
    y = DEEPNORM_ALPHA * x + _dot(_stream_tile(op_ref, os_ref, prompt_tiles), w_ref[...])
    y_ref[...] = _layer_norm(y, g_ref[...], b_ref[...])


def _out_ln(o_p, o_s, w_o, x, g, b):
    tm = TOKEN_TILE
    n_p = o_p.shape[0] // tm
    n = o_p.shape[0] + o_s.shape[0]
    row = pl.BlockSpec((tm, D_MODEL), lambda i: (i, 0))
    full = lambda a: pl.BlockSpec(a.shape, lambda i: (0, 0))
    xs = list(x) if isinstance(x, tuple) else [x]
    x_specs = _stream_specs(tm, D_MODEL, n_p) if isinstance(x, tuple) else [row]
    return pl.pallas_call(
        functools.partial(_out_ln_kernel, prompt_tiles=n_p),
        grid=(n // tm,),
        in_specs=_stream_specs(tm, D_MODEL, n_p) + [full(w_o)] + x_specs + [full(g), full(b)],
        out_specs=row,
        out_shape=jax.ShapeDtypeStruct((n, D_MODEL), F32),
        compiler_params=_params("parallel"),
        name="out_proj_ln",
    )(o_p, o_s, w_o, *xs, g, b)


def _band_qkv_kernel(x_ref, wqk_ref, wv_ref, wvt_ref, q_ref, k_ref, vt_ref, kf_ref, vf_ref, *,
                     first_keep_tile):
    xb = x_ref[...].astype(BF16)
    qk = _dot(xb, wqk_ref[...])
    scale = math.log2(math.e) / math.sqrt(BAND_HEAD_DIM)
    q_ref[...] = (qk[:, :D_MODEL] * scale).astype(BF16)
    k = qk[:, D_MODEL:]
    k_ref[...] = k.astype(BF16)
    vt = _dot_nt(wvt_ref[...], xb).astype(BF16)
    for t in range(vt_ref.shape[0]):
        vt_ref[t] = vt[:, LANES * t:LANES * (t + 1)]

    @pl.when(pl.program_id(0) >= first_keep_tile)
    def _():
        kf_ref[...] = k
        vf_ref[...] = _dot(xb, wv_ref[...])


def _band_qkv(x, w_qkv, n_keep):
    n = x.shape[0]
    tm = TOKEN_TILE
    first = (n - n_keep) // tm
    w = w_qkv.astype(BF16)
    wqk, wv = w[:, :2 * D_MODEL], w[:, 2 * D_MODEL:]
    row = pl.BlockSpec((tm, D_MODEL), lambda i: (i, 0))
    keep = pl.BlockSpec((tm, D_MODEL), lambda i: (jnp.maximum(i - first, 0), 0))
    full = lambda a: pl.BlockSpec(a.shape, lambda i: (0, 0))
    bf = jax.ShapeDtypeStruct((n, D_MODEL), BF16)
    kf = jax.ShapeDtypeStruct((n_keep, D_MODEL), F32)
    return pl.pallas_call(
        functools.partial(_band_qkv_kernel, first_keep_tile=first),
        grid=(n // tm,),
        in_specs=[row, full(wqk), full(wv), full(wv)],
        out_specs=[row, row, pl.BlockSpec((tm // LANES, D_MODEL, LANES), lambda i: (i, 0, 0)),
                   keep, keep],
        out_shape=[bf, bf, jax.ShapeDtypeStruct((n // LANES, D_MODEL, LANES), BF16), kf, kf],
        compiler_params=_params("arbitrary"),
        name="band_qkv",
    )(x, wqk, wv, wv.T)


def _band_prompt_attn_kernel(q_ref, k_ref, vt_ref, rel_ref, o_ref, bias_sc):
    i = pl.program_id(1)
    half = LANES // 2
    low = lax.broadcasted_iota(jnp.int32, (1, LANES), 1) < half
    n_win = BAND_WIN // LANES
    n_left = BAND_KEEP_MAX // LANES

    @pl.when(i == 0)
    def _():
        k_in = lax.broadcasted_iota(jnp.int32, (LANES, LANES), 0)
        q_in = lax.broadcasted_iota(jnp.int32, (LANES, LANES), 1)
        for t in range(n_left + 1):
            q_chunk = (BAND_QSUB * t + q_in) // CHUNK
            for kb in range(n_win):
                k_chunk = (LANES * kb + k_in) // CHUNK
                visible = (k_chunk <= q_chunk) & (k_chunk >= q_chunk - LEFT_CHUNKS)
                start = LANES * (t - kb + n_left)
                for h in range(2):
                    run = jnp.broadcast_to(rel_ref[h, :, start:start + 2 * LANES], (LANES, 2 * LANES))
                    tile = pltpu.roll(run, 0, 1, stride=1, stride_axis=0)[:, LANES:]
                    bias_sc[t, LANES * kb:LANES * (kb + 1), LANES * h:LANES * (h + 1)] = jnp.where(
                        visible, tile, NEG_INF)

    def body(g, c):
        work = []
        for u in range(BAND_GROUP):
            sb = g * BAND_GROUP + u
            blk = i * (BAND_QBLOCK // BAND_QSUB) + sb
            b0 = jnp.maximum(blk - n_left, 0)
            rows = pl.ds(pl.multiple_of(sb * BAND_QSUB, BAND_QSUB), BAND_QSUB)
            q = q_ref[rows, :]
            zero = jnp.zeros_like(q)
            qq = jnp.concatenate([jnp.where(low, q, zero), jnp.where(low, zero, q)], axis=0)
            kw = k_ref[pl.ds(pl.multiple_of(b0 * LANES, LANES), BAND_WIN), :]
            s = _dot_nt(kw, qq) + bias_sc[jnp.minimum(blk, n_left)]
            work.append((rows, b0, s))
        for rows, b0, s in work:
            m = jnp.max(s, 0, keepdims=True)
            p = jnp.exp2(s - m)
            l = jnp.sum(p, 0, keepdims=True)
            pb = p.astype(BF16)
            vt = jnp.concatenate([vt_ref[b0 + w] for w in range(n_win)], axis=1)
            o_t = jnp.concatenate(
                [_dot(vt[half * h:half * (h + 1)], pb[:, LANES * h:LANES * (h + 1)])
                 / l[:, LANES * h:LANES * (h + 1)] for h in range(2)], axis=0)
            o_ref[rows, :] = o_t.T.astype(o_ref.dtype)
        return c

    lax.fori_loop(0, BAND_QBLOCK // BAND_QSUB // BAND_GROUP, body, 0)


def _band_prompt_attn(q, k, vt, rel_table, s):
    n_left = BAND_KEEP_MAX // LANES
    offs = np.arange(LANES * (2 * n_left + 2)) - LANES * (n_left + 1)
    rel_run = (rel_table.astype(F32)[:, np.clip(offs, -REL_MAX, REL_MAX) + REL_MAX]
               * math.log2(math.e))[:, None, :]
    return pl.pallas_call(
        _band_prompt_attn_kernel,
        grid=(N_PAIRS, s // BAND_QBLOCK),
        in_specs=[pl.BlockSpec((BAND_QBLOCK, LANES), lambda j, i: (i, j)),
                  pl.BlockSpec((s, LANES), lambda j, i: (0, j)),
                  pl.BlockSpec((s // LANES, LANES, LANES), lambda j, i: (0, j, 0)),
                  pl.BlockSpec((2, 1, rel_run.shape[-1]), lambda j, i: (j, 0, 0))],
        out_specs=pl.BlockSpec((BAND_QBLOCK, LANES), lambda j, i: (i, j)),
        out_shape=jax.ShapeDtypeStruct((s, D_MODEL), BF16),
        scratch_shapes=[pltpu.VMEM((n_left + 1, BAND_WIN, 2 * LANES), F32)],
        compiler_params=_params("parallel", "arbitrary"),
        name="band_prompt_attn",
    )(q, k, vt, rel_run)


def _band_sample_attn_kernel(q_ref, kc_ref, vc_ref, kn_ref, vn_ref, bc_ref, bn_ref, o_ref):
    lane = lax.broadcasted_iota(jnp.int32, (1, LANES), 1)
    low = lane < LANES // 2
    cols = [slice(LANES * j, LANES * (j + 1)) for j in range(N_PAIRS)]
    scores = []
    for j in range(N_PAIRS):
        q = q_ref[0, :, cols[j]]
        kc = kc_ref[0, :, cols[j]].astype(BF16)
        kn = kn_ref[0, :, cols[j]].astype(BF16)
        for h in range(2):
            qh = jnp.where(low if h == 0 else ~low, q, jnp.zeros_like(q))
            scores.append((_dot_nt(qh, kc) + bc_ref[2 * j + h], _dot_nt(qh, kn) + bn_ref[2 * j + h]))
    probs = []
    for sc, sn in scores:
        m = jnp.maximum(jnp.max(sc, -1, keepdims=True), jnp.max(sn, -1, keepdims=True))
        pc = jnp.exp2(sc - m)
        pn = jnp.exp2(sn - m)
        l = jnp.sum(pc, -1, keepdims=True) + jnp.sum(pn, -1, keepdims=True)
        probs.append((pc.astype(BF16), pn.astype(BF16), l))
    for j in range(N_PAIRS):
        vc = vc_ref[0, :, cols[j]].astype(BF16)
        vn = vn_ref[0, :, cols[j]].astype(BF16)
        outs = [(_dot(pc, vc) + _dot(pn, vn)) / l for pc, pn, l in probs[2 * j:2 * j + 2]]
        o_ref[0, :, cols[j]] = jnp.where(low, outs[0], outs[1]).astype(o_ref.dtype)


def _band_sample_attn(q, kc, vc, kn, vn, bias_c, bias_n):
    b, t, _ = q.shape
    blk = lambda a: pl.BlockSpec((1,) + a.shape[1:], lambda i: (i, 0, 0))
    full = lambda a: pl.BlockSpec(a.shape, lambda i: (0, 0, 0))
    return pl.pallas_call(
        _band_sample_attn_kernel,
        grid=(b,),
        in_specs=[blk(q), blk(kc), blk(vc), blk(kn), blk(vn), full(bias_c), full(bias_n)],
        out_specs=pl.BlockSpec((1, t, D_MODEL), lambda i: (i, 0, 0)),
        out_shape=jax.ShapeDtypeStruct((b, t, D_MODEL), BF16),
        compiler_params=_params("parallel"),
        name="band_sample_attn",
    )(q, kc, vc, kn, vn, bias_c, bias_n)


def _band_bias(rel_table, q_pos, k_pos):
    nq, nk = len(q_pos), len(k_pos)
    assert (np.diff(q_pos) == 1).all() and (np.diff(k_pos) == 1).all()
    span = nq + nk - 1
    diag = int(q_pos[0] - k_pos[0]) + nq - 1 - np.arange(span)
    u = rel_table.astype(F32)[:, np.clip(diag, -REL_MAX, REL_MAX) + REL_MAX] * math.log2(math.e)
    u = jnp.pad(u, ((0, 0), (0, 1)))
    skew = jnp.tile(u, (1, nq))[:, :nq * span].reshape(-1, nq, span)
    bias = skew[:, :, nq - 1:nq - 1 + nk]
    vis = _static_visible(q_pos, k_pos, LEFT_CHUNKS)
    return jnp.where(jnp.asarray(vis)[None], bias, NEG_INF)


def _route(logits_t, rb_ref):
    scores = jax.nn.sigmoid(logits_t)
    rows = [scores[e:e + 1, :] for e in range(N_EXPERTS)]
    biased = [rows[e] + rb_ref[e] for e in range(N_EXPERTS)]
    best = None
    sel = None
    for g in range(N_GROUPS):
        a, b, c, d = biased[EXPERTS_PER_GROUP * g:EXPERTS_PER_GROUP * (g + 1)]
        top2 = jnp.maximum(jnp.maximum(jnp.maximum(a + b, a + c), jnp.maximum(a + d, b + c)),
                           jnp.maximum(b + d, c + d))
        if g == 0:
            best, sel = top2, jnp.zeros(top2.shape, jnp.int32)
        else:
            better = top2 > best
            sel = jnp.where(better, g, sel)
            best = jnp.where(better, top2, best)
    cand = [jnp.where(sel == e // EXPERTS_PER_GROUP, biased[e], NEG_INF) for e in range(N_EXPERTS)]

    def argmax_first(vals):
        top = functools.reduce(jnp.maximum, vals)
        idx = jnp.full(top.shape, N_EXPERTS, jnp.int32)
        for e in reversed(range(N_EXPERTS)):
            idx = jnp.where(vals[e] == top, e, idx)
        return idx

    i1 = argmax_first(cand)
    i2 = argmax_first([jnp.where(i1 == e, -jnp.inf, cand[e]) for e in range(N_EXPERTS)])
    picked = [jnp.where((i1 == e) | (i2 == e), rows[e], 0.0) for e in range(N_EXPERTS)]
    total = functools.reduce(jnp.add, picked)
    return [p / total for p in picked]


def _moe_kernel(x_ref, rw_ref, rb_ref, wg_ref, wu_ref, wd_ref, g_ref, b_ref, *refs, prompt_tiles):
    *y_refs, xb_sc, comb_sc, acc_sc = refs
    grp = pl.program_id(1)

    def hidden(xb):
        return [jax.nn.silu(_dot(xb, wg_ref[e])) * _dot(xb, wu_ref[e])
                for e in range(EXPERTS_PER_GROUP)]

    def combine(hs, comb):
        parts = [(h * comb[:, e:e + 1]).astype(BF16) for e, h in enumerate(hs)]
        return _dot(jnp.concatenate(parts, axis=1), wd_ref[0])

    @pl.when(grp == 0)
    def _():
        x = x_ref[...]
        xb = x.astype(BF16)
        xb_sc[...] = xb
        hs = hidden(xb)
        x_lo = (x - xb.astype(F32)).astype(BF16)
        rw_hi = rw_ref[0]
        logits_t = _dot_nt(rw_hi, xb) + (_dot_nt(rw_hi, x_lo) + _dot_nt(rw_ref[1], xb))
        comb_rows = _route(logits_t, rb_ref)
        comb = jnp.concatenate(comb_rows, axis=0).T
        for g in range(N_GROUPS):
            comb_sc[g] = comb[:, EXPERTS_PER_GROUP * g:EXPERTS_PER_GROUP * (g + 1)]
        acc_sc[...] = DEEPNORM_ALPHA * x + combine(hs, comb[:, :EXPERTS_PER_GROUP])

    @pl.when(grp > 0)
    def _():
        acc_sc[...] += combine(hidden(xb_sc[...]), comb_sc[grp])

    @pl.when(grp == N_GROUPS - 1)
    def _():
        y = _layer_norm(acc_sc[...], g_ref[...], b_ref[...])
        if len(y_refs) == 1:
            y_refs[0][...] = y
        else:
            is_prompt = pl.program_id(0) < prompt_tiles

            @pl.when(is_prompt)
            def _():
                y_refs[0][...] = y

            @pl.when(jnp.logical_not(is_prompt))
            def _():
                y_refs[1][...] = y


def _moe_ln(x, rw_t, rb, wg, wu, wd, layer, g, b, n_prompt=None):
    n = x.shape[0]
    tm = MOE_TILE
    row = pl.BlockSpec((tm, D_MODEL), lambda i, j: (i, 0))
    full = lambda a: pl.BlockSpec(a.shape, lambda i, j: (0, 0))
    grp = lambda a: pl.BlockSpec((None, a.shape[1] // N_GROUPS) + a.shape[2:],
                                 lambda i, j: (layer, j, 0, 0))
    if n_prompt is None:
        n_p = n // tm
        out_specs = row
        out_shape = jax.ShapeDtypeStruct((n, D_MODEL), F32)
        semantics = ("parallel", "arbitrary")
    else:
        n_p = n_prompt // tm
        out_specs = _stream_specs(tm, D_MODEL, n_p)
        out_shape = [jax.ShapeDtypeStruct((n_prompt, D_MODEL), F32),
                     jax.ShapeDtypeStruct((n - n_prompt, D_MODEL), F32)]
        semantics = ("arbitrary", "arbitrary")
    return pl.pallas_call(
        functools.partial(_moe_kernel, prompt_tiles=n_p),
        grid=(n // tm, N_GROUPS),
        in_specs=[row, pl.BlockSpec(rw_t.shape, lambda i, j: (0, 0, 0)),
                  pl.BlockSpec(memory_space=pltpu.SMEM),
                  grp(wg), grp(wu), grp(wd), full(g), full(b)],
        out_specs=out_specs,
        out_shape=out_shape,
        scratch_shapes=[pltpu.VMEM((tm, D_MODEL), BF16),
                        pltpu.VMEM((N_GROUPS, tm, EXPERTS_PER_GROUP), F32),
                        pltpu.VMEM((tm, D_MODEL), F32)],
        compiler_params=_params(*semantics),
        name="moe_ln",
    )(x, rw_t, rb, wg, wu, wd, g, b)


def _rope_tables():
    half = MLA_ROPE // 2
    inv = ROPE_THETA ** (-jnp.arange(half, dtype=F32) / half)
    pos = jnp.concatenate([jnp.arange(SEQ, dtype=jnp.int32),
                           jnp.tile(PAST_LEN + jnp.arange(DEC_SEQ, dtype=jnp.int32), DEC_BATCH)])
    ang = pos.astype(F32)[:, None] * inv[None, :]
    reps = LANES // half
    return jnp.tile(jnp.cos(ang), (1, reps)), jnp.tile(jnp.sin(ang), (1, reps))


def _rot_cols(w):
    half = MLA_ROPE // 2
    return jnp.concatenate([-w[..., half:], w[..., :half]], -1)


def _mla_weights(w_in, w_uq, w_uk, w_uv, w_o):
    o_kv = MLA_Q_LORA + MLA_KV_LORA
    w_r = w_in[:, o_kv:]
    zpad = jnp.zeros((D_MODEL, LANES - 2 * MLA_ROPE), F32)
    w_in_p = jnp.concatenate([w_in[:, :o_kv], w_r, w_r, zpad,
                              _rot_cols(w_r), _rot_cols(w_r), zpad], 1).astype(BF16)
    wq = w_uq.reshape(MLA_Q_LORA, N_PAIRS, 2, MLA_QK)
    nope = wq[..., :MLA_NOPE].reshape(MLA_Q_LORA, N_PAIRS, 2 * MLA_NOPE)
    ropew = wq[..., MLA_NOPE:]
    zq = jnp.zeros((MLA_Q_LORA, N_PAIRS, LANES - 2 * MLA_ROPE), F32)
    rope_cols = jnp.concatenate([ropew.reshape(MLA_Q_LORA, N_PAIRS, 2 * MLA_ROPE), zq], -1)
    rot_cols = jnp.concatenate([_rot_cols(ropew).reshape(MLA_Q_LORA, N_PAIRS, 2 * MLA_ROPE), zq], -1)
    wqa = jnp.concatenate([nope, rope_cols], -1).reshape(MLA_Q_LORA, 2 * LANES * N_PAIRS).astype(BF16)
    wqb = rot_cols.reshape(MLA_Q_LORA, LANES * N_PAIRS).astype(BF16)
    return w_in_p, wqa, wqb, w_uk.astype(BF16), w_uv.astype(BF16), w_o.astype(BF16)


def _group_experts(w_gate, w_up, w_down):
    wd = w_down.reshape(DEPTH, N_GROUPS, EXPERTS_PER_GROUP * D_EXPERT, D_MODEL).astype(BF16)
    return w_gate.astype(BF16), w_up.astype(BF16), wd


def kernel(x_prompt, x_sample, cache_mla_ckv, cache_mla_krope, cache_band_k, cache_band_v,
           mla_w_in, mla_q_norm, mla_kv_norm, mla_w_uq, mla_w_uk, mla_w_uv, mla_w_o,
           band_w_qkv, band_rel_bias, band_w_o,
           router_w, router_b, moe_w_gate, moe_w_up, moe_w_down,
           ln_mix_g, ln_mix_b, ln_ffn_g, ln_ffn_b):
    x = (x_prompt.reshape(SEQ, D_MODEL), x_sample.reshape(N_SAMPLE, D_MODEL))
    wg, wu, wd = _group_experts(moe_w_gate, moe_w_up, moe_w_down)
    cos, sin = _rope_tables()
    rw_f = router_w.T.astype(F32)
    rw_hi = rw_f.astype(BF16)
    rw_t = jnp.stack([rw_hi, (rw_f - rw_hi.astype(F32)).astype(BF16)], 0)
    rb = router_b.astype(F32)
    band_keep = cache_band_k.shape[2]
    row2 = lambda a: a.reshape(1, -1)

    ckv_p, kr_p, ckv_s, kr_s = [], [], [], []
    bk_p, bv_p, bk_s, bv_s = [], [], [], []
    for i in range(DEPTH):
        j = i // N_MIXERS
        if i % N_MIXERS == 0:
            w_in_p, wqa, wqb, wuk, wuv, w_o = _mla_weights(
                mla_w_in[j], mla_w_uq[j], mla_w_uk[j], mla_w_uv[j], mla_w_o[j])
            x_pair = x if isinstance(x, tuple) else (x[:SEQ], x[SEQ:])
            ckv, kr128, q = _mla_proj(*x_pair, w_in_p, row2(mla_q_norm[j]), row2(mla_kv_norm[j]),
                                      wqa, wqb, cos, sin)
            kr = kr128[:, :MLA_ROPE]
            ckv_p.append(ckv[:SEQ].reshape(1, SEQ, MLA_KV_LORA))
            kr_p.append(kr[:SEQ].reshape(1, SEQ, MLA_ROPE))
            ckv_s.append(ckv[SEQ:].reshape(DEC_BATCH, DEC_SEQ, MLA_KV_LORA))
            kr_s.append(kr[SEQ:].reshape(DEC_BATCH, DEC_SEQ, MLA_ROPE))
            k_p, vt_p = _kv_expand(ckv, kr128, wuk, wuv.T, SEQ)
            o_p = _mla_prompt_attn(q, k_p, vt_p)
            cache_kr = cache_mla_krope[j].astype(BF16)
            cache_kr128 = jnp.concatenate(
                [cache_kr, cache_kr, jnp.zeros(cache_kr.shape[:2] + (LANES - 2 * MLA_ROPE,), BF16)], -1)
            o_s = _mla_sample_attn(q[SEQ:].reshape(DEC_BATCH, DEC_SEQ, -1), cache_mla_ckv[j], cache_kr128,
                                   ckv_s[-1], kr128[SEQ:].reshape(DEC_BATCH, DEC_SEQ, LANES), wuk.T, wuv)
        else:
            n_keep = min(BAND_KEEP_MAX, SEQ) + N_SAMPLE
            x_all = jnp.concatenate(x, 0) if isinstance(x, tuple) else x
            q, k, vt, k_f, v_f = _band_qkv(x_all, band_w_qkv[j], n_keep)
            keep = n_keep - N_SAMPLE
            hd = (BAND_HEADS, BAND_HEAD_DIM)
            bk_p.append(k_f[:keep].reshape((1, keep) + hd))
            bv_p.append(v_f[:keep].reshape((1, keep) + hd))
            bk_s.append(k_f[keep:].reshape((DEC_BATCH, DEC_SEQ) + hd))
            bv_s.append(v_f[keep:].reshape((DEC_BATCH, DEC_SEQ) + hd))
            w_o = band_w_o[j].astype(BF16)
            o_p = _band_prompt_attn(q, k, vt, band_rel_bias[j], SEQ)
            q_pos = PAST_LEN + np.arange(DEC_SEQ)
            pos_c = PAST_LEN - band_keep + np.arange(band_keep)
            bias_c = _band_bias(band_rel_bias[j], q_pos, pos_c)
            bias_n = _band_bias(band_rel_bias[j], q_pos, q_pos)
            s3 = lambda a: a.reshape(DEC_BATCH, DEC_SEQ, D_MODEL)
            cache3 = lambda a: a.reshape(DEC_BATCH, band_keep, D_MODEL)
            o_s = _band_sample_attn(s3(q[SEQ:]), cache3(cache_band_k[j]), cache3(cache_band_v[j]),
                                    s3(k_f[keep:]), s3(v_f[keep:]), bias_c, bias_n)
        x = _out_ln(o_p, o_s.reshape(N_SAMPLE, D_MODEL), w_o, x, row2(ln_mix_g[i]), row2(ln_mix_b[i]))
        x = _moe_ln(x, rw_t, rb, wg, wu, wd, i, row2(ln_ffn_g[i]), row2(ln_ffn_b[i]),
                    n_prompt=SEQ if i == DEPTH - 1 else None)

    return (x[0].reshape(1, SEQ, D_MODEL), x[1].reshape(DEC_BATCH, DEC_SEQ, D_MODEL),
            jnp.stack(ckv_p, 0), jnp.stack(kr_p, 0), jnp.stack(ckv_s, 0), jnp.stack(kr_s, 0),
            jnp.stack(bk_p, 0), jnp.stack(bv_p, 0), jnp.stack(bk_s, 0), jnp.stack(bv_s, 0))
```

```python
import functools
import math

import numpy as np
import jax
import jax.numpy as jnp
from jax import lax
from jax.experimental import pallas as pl
from jax.experimental.pallas import tpu as pltpu

D_MODEL = 1024
SEQ = 16384
DEPTH = 2
DEC_BATCH = 32
DEC_SEQ = 32
PAST_LEN = 1024
CHUNK = 64
N_MIXERS = 2

MLA_HEADS = 16
MLA_Q_LORA = 384
MLA_KV_LORA = 256
MLA_NOPE = 64
MLA_ROPE = 32
MLA_QK = MLA_NOPE + MLA_ROPE
MLA_V = 64
ROPE_THETA = 10000.0

BAND_HEADS = 16
BAND_HEAD_DIM = D_MODEL // BAND_HEADS
LEFT_CHUNKS = 8
BAND_KEEP_MAX = LEFT_CHUNKS * CHUNK
REL_MAX = 128

N_EXPERTS = 16
N_GROUPS = 4
EXPERTS_PER_GROUP = N_EXPERTS // N_GROUPS
D_EXPERT = 256

DEEPNORM_ALPHA = (2.0 * DEPTH) ** 0.25
NORM_EPS = 1e-5
NEG_INF = -1e30

N_SAMPLE = DEC_BATCH * DEC_SEQ
N_TOKENS = SEQ + N_SAMPLE
N_PAIRS = MLA_HEADS // 2
LANES = 128
VMEM_LIMIT = 56 * 1024 * 1024

TOKEN_TILE = 512
MOE_TILE = 1024
ATTN_TQ = 1024
ATTN_TK = 1024
ONES_ROWS = 16
BAND_QSUB = 128
BAND_WIN = BAND_KEEP_MAX + BAND_QSUB
SAMPLE_GROUP = 4
BAND_QBLOCK = 2048
BAND_GROUP = 16

BF16 = jnp.bfloat16
F32 = jnp.float32


def _params(*sem):
    return pltpu.CompilerParams(dimension_semantics=sem, vmem_limit_bytes=VMEM_LIMIT)


def _dot(a, b):
    return jnp.dot(a, b, preferred_element_type=F32)


def _dot_nt(a, b):
    return lax.dot_general(a, b, (((1,), (1,)), ((), ())), preferred_element_type=F32)


def _layer_norm(y, g, b):
    mu = jnp.mean(y, -1, keepdims=True)
    d = y - mu
    var = jnp.mean(d * d, -1, keepdims=True)
    return d * lax.rsqrt(var + NORM_EPS) * g + b


def _rms_norm(y, g):
    return y * lax.rsqrt(jnp.mean(y * y, -1, keepdims=True) + NORM_EPS) * g


def _stream_specs(tm, width, prompt_tiles):
    return [pl.BlockSpec((tm, width), lambda i, *_: (jnp.minimum(i, prompt_tiles - 1), 0)),
            pl.BlockSpec((tm, width), lambda i, *_: (jnp.maximum(i - prompt_tiles, 0), 0))]


def _stream_tile(p_ref, s_ref, prompt_tiles):
    return jnp.where(pl.program_id(0) < prompt_tiles, p_ref[...], s_ref[...])


def _mla_proj_kernel(xp_ref, xs_ref, w_in_ref, qn_ref, kvn_ref, wqa_ref, wqb_ref, cos_ref, sin_ref,
                     ckv_ref, kr_ref, q_ref, *, prompt_tiles):
    xb = _stream_tile(xp_ref, xs_ref, prompt_tiles).astype(BF16)
    a = _dot(xb, w_in_ref[...])
    cq = _rms_norm(a[:, :MLA_Q_LORA], qn_ref[...])
    o_kv = MLA_Q_LORA + MLA_KV_LORA
    ckv_ref[...] = _rms_norm(a[:, MLA_Q_LORA:o_kv], kvn_ref[...])
    cos = cos_ref[...]
    sin = sin_ref[...]
    kr_ref[...] = a[:, o_kv:o_kv + LANES] * cos + a[:, o_kv + LANES:] * sin
    cqb = cq.astype(BF16)
    qa = _dot(cqb, wqa_ref[...])
    qb = _dot(cqb, wqb_ref[...])
    scale = math.log2(math.e) / math.sqrt(MLA_QK)
    for j in range(N_PAIRS):
        c0 = 2 * LANES * j
        q_ref[:, c0:c0 + LANES] = (qa[:, c0:c0 + LANES] * scale).astype(BF16)
        rot = qa[:, c0 + LANES:c0 + 2 * LANES] * cos + qb[:, LANES * j:LANES * (j + 1)] * sin
        q_ref[:, c0 + LANES:c0 + 2 * LANES] = (rot * scale).astype(BF16)


def _mla_proj(x_p, x_s, w_in_p, qn, kvn, wqa, wqb, cos, sin):
    n = x_p.shape[0] + x_s.shape[0]
    tm = TOKEN_TILE
    n_p = x_p.shape[0] // tm
    row = lambda w: pl.BlockSpec((tm, w), lambda i: (i, 0))
    full = lambda a: pl.BlockSpec(a.shape, lambda i: (0, 0))
    return pl.pallas_call(
        functools.partial(_mla_proj_kernel, prompt_tiles=n_p),
        grid=(n // tm,),
        in_specs=_stream_specs(tm, D_MODEL, n_p) + [full(w_in_p), full(qn), full(kvn), full(wqa),
                                                    full(wqb), row(LANES), row(LANES)],
        out_specs=[row(MLA_KV_LORA), row(LANES), row(2 * LANES * N_PAIRS)],
        out_shape=[jax.ShapeDtypeStruct((n, MLA_KV_LORA), F32),
                   jax.ShapeDtypeStruct((n, LANES), F32),
                   jax.ShapeDtypeStruct((n, 2 * LANES * N_PAIRS), BF16)],
        compiler_params=_params("parallel"),
        name="mla_proj",
    )(x_p, x_s, w_in_p, qn, kvn, wqa, wqb, cos, sin)


def _kv_expand_kernel(ckv_ref, kr_ref, wuk_ref, wuvt_ref, k_ref, vt_ref):
    cb = ckv_ref[...].astype(BF16)
    kn = _dot(cb, wuk_ref[...])
    vt_ref[0] = _dot_nt(wuvt_ref[...], cb).astype(BF16)
    krb = kr_ref[...].astype(BF16)
    for j in range(N_PAIRS):
        c0 = 2 * LANES * j
        k_ref[:, c0:c0 + LANES] = kn[:, LANES * j:LANES * (j + 1)].astype(BF16)
        k_ref[:, c0 + LANES:c0 + 2 * LANES] = krb


def _kv_expand(ckv, kr128, wuk, wuv_t, n):
    tm = ATTN_TK
    row = lambda w: pl.BlockSpec((tm, w), lambda i: (i, 0))
    full = lambda a: pl.BlockSpec(a.shape, lambda i: (0, 0))
    nv = LANES * N_PAIRS
    return pl.pallas_call(
        _kv_expand_kernel,
        grid=(n // tm,),
        in_specs=[row(MLA_KV_LORA), row(LANES), full(wuk), full(wuv_t)],
        out_specs=[row(2 * LANES * N_PAIRS), pl.BlockSpec((1, nv, tm), lambda i: (i, 0, 0))],
        out_shape=[jax.ShapeDtypeStruct((n, 2 * LANES * N_PAIRS), BF16),
                   jax.ShapeDtypeStruct((n // tm, nv, tm), BF16)],
        compiler_params=_params("parallel"),
        name="mla_kv_expand",
    )(ckv, kr128, wuk, wuv_t)


def _pair_query_masks(width):
    lane = lax.broadcasted_iota(jnp.int32, (1, width), 1)
    half = LANES // 2
    r0 = LANES
    m0 = (lane < half) | ((lane >= r0) & (lane < r0 + MLA_ROPE))
    m1 = ((lane >= half) & (lane < LANES)) | ((lane >= r0 + MLA_ROPE) & (lane < r0 + 2 * MLA_ROPE))
    return m0, m1


def _mla_prompt_attn_kernel(q_ref, k_ref, vt_ref, o_ref, qh_sc, s_sc, bmax_sc, m_sc, acc_sc):
    i = pl.program_id(1)
    tq, tk = ATTN_TQ, ATTN_TK
    half = LANES // 2
    q = q_ref[...]
    for h, mask in enumerate(_pair_query_masks(2 * LANES)):
        qh_sc[h] = jnp.where(mask, q, jnp.zeros_like(q))
        m_sc[h] = jnp.full((1, tq), NEG_INF, F32)
        acc_sc[h] = jnp.zeros((half + ONES_ROWS, tq), F32)
    ones = jnp.ones((ONES_ROWS, tk), BF16)

    def scores(kb, h, slot):
        k = k_ref[pl.ds(pl.multiple_of(kb * tk, tk), tk), :]
        s = _dot_nt(k, qh_sc[h])
        s_sc[h, slot] = s
        bmax_sc[h, slot] = jnp.max(s, 0, keepdims=True)

    def update(kb, h, slot, visible=None):
        s = s_sc[h, slot]
        if visible is None:
            block_max = bmax_sc[h, slot]
        else:
            s = jnp.where(visible, s, NEG_INF)
            block_max = jnp.max(s, 0, keepdims=True)
        m_prev = m_sc[h]
        m_new = jnp.maximum(m_prev, block_max)
        alpha = jnp.exp2(m_prev - m_new)
        p = jnp.exp2(s - m_new).astype(BF16)
        vt = jnp.concatenate([vt_ref[kb, half * h:half * (h + 1), :], ones], axis=0)
        acc_sc[h] = alpha * acc_sc[h] + _dot(vt, p)
        m_sc[h] = m_new

    def step(next_kb, kb, slot, visible=None):
        if next_kb is not None:
            for h in range(2):
                scores(next_kb, h, 1 - slot)
        for h in range(2):
            update(kb, h, slot, visible)

    for h in range(2):
        scores(i, h, 0)
    k_chunk = lax.broadcasted_iota(jnp.int32, (tk, tq), 0) // CHUNK
    q_chunk = lax.broadcasted_iota(jnp.int32, (tk, tq), 1) // CHUNK
    step(0, i, 0, k_chunk <= q_chunk)

    def body(p, c):
        step(2 * p + 1, 2 * p, 1)
        step(2 * p + 2, 2 * p + 1, 0)
        return c

    lax.fori_loop(0, (i - 1) // 2, body, 0)

    @pl.when(i % 2 == 1)
    def _():
        step(None, i - 1, 1)

    @pl.when((i % 2 == 0) & (i > 0))
    def _():
        step(i - 1, i - 2, 1)
        step(None, i - 1, 0)

    o_t = jnp.concatenate([acc_sc[h, :half] / acc_sc[h, half:half + 1] for h in range(2)], axis=0)
    o_ref[...] = o_t.T.astype(o_ref.dtype)


def _mla_prompt_attn(q, k, vt):
    s = k.shape[0]
    tq = ATTN_TQ
    assert ATTN_TQ == ATTN_TK and vt.shape == (s // ATTN_TK, LANES * N_PAIRS, ATTN_TK)
    return pl.pallas_call(
        _mla_prompt_attn_kernel,
        grid=(N_PAIRS, s // tq),
        in_specs=[pl.BlockSpec((tq, 2 * LANES), lambda j, i: (i, j)),
                  pl.BlockSpec((s, 2 * LANES), lambda j, i: (0, j)),
                  pl.BlockSpec((s // ATTN_TK, LANES, ATTN_TK), lambda j, i: (0, j, 0))],
        out_specs=pl.BlockSpec((tq, LANES), lambda j, i: (i, j)),
        out_shape=jax.ShapeDtypeStruct((s, LANES * N_PAIRS), BF16),
        scratch_shapes=[pltpu.VMEM((2, tq, 2 * LANES), BF16),
                        pltpu.VMEM((2, 2, ATTN_TK, tq), F32),
                        pltpu.VMEM((2, 2, 1, tq), F32),
                        pltpu.VMEM((2, 1, tq), F32),
                        pltpu.VMEM((2, LANES // 2 + ONES_ROWS, tq), F32)],
        compiler_params=_params("parallel", "arbitrary"),
        name="mla_prompt_attn",
    )(q, k, vt)


def _static_visible(q_pos, k_pos, left_chunks=None):
    qc = (q_pos // CHUNK)[:, None]
    kc = (k_pos // CHUNK)[None, :]
    vis = kc <= qc
    if left_chunks is not None:
        vis = vis & (kc >= qc - left_chunks) & (k_pos[None, :] >= 0)
    return vis


def _mla_sample_attn_kernel(q_ref, cc_ref, rc_ref, cn_ref, rn_ref, wukt_ref, wuv_ref, o_ref, *,
                            visible):
    nope_masks = _pair_query_masks(2 * LANES)
    lane = lax.broadcasted_iota(jnp.int32, (1, LANES), 1)
    low = lane < LANES // 2
    batches = range(q_ref.shape[0])
    t = q_ref.shape[1]
    queries = []
    for b in batches:
        q_lat, q_rope = [], []
        for j in range(N_PAIRS):
            q = q_ref[b, :, 2 * LANES * j:2 * LANES * (j + 1)]
            for h in range(2):
                qm = jnp.where(nope_masks[h], q, jnp.zeros_like(q))
                q_lat.append(_dot(qm[:, :LANES], wukt_ref[LANES * j:LANES * (j + 1), :]).astype(BF16))
                q_rope.append(qm[:, LANES:])
        queries.append((jnp.concatenate(q_lat, axis=0), jnp.concatenate(q_rope, axis=0)))
    latents = [(cc_ref[b].astype(BF16), cn_ref[b].astype(BF16)) for b in batches]
    scores = []
    for b in batches:
        (q_lat, q_rope), (cc, cn) = queries[b], latents[b]
        s_c = _dot_nt(q_lat, cc) + _dot_nt(q_rope, rc_ref[b])
        s_n = _dot_nt(q_lat, cn) + _dot_nt(q_rope, rn_ref[b].astype(BF16))
        if visible is not None:
            vis_c, vis_n = (jnp.asarray(np.tile(v, (MLA_HEADS, 1))) for v in visible)
            s_c = jnp.where(vis_c, s_c, NEG_INF)
            s_n = jnp.where(vis_n, s_n, NEG_INF)
        scores.append((s_c, s_n))
    probs = []
    for s_c, s_n in scores:
        m = jnp.maximum(jnp.max(s_c, -1, keepdims=True), jnp.max(s_n, -1, keepdims=True))
        p_c = jnp.exp2(s_c - m)
        p_n = jnp.exp2(s_n - m)
        l = jnp.sum(p_c, -1, keepdims=True) + jnp.sum(p_n, -1, keepdims=True)
        probs.append((p_c.astype(BF16), p_n.astype(BF16), l))
    o_lats = [((_dot(p_c, cc) + _dot(p_n, cn)) / l).astype(BF16)
              for (p_c, p_n, l), (cc, cn) in zip(probs, latents)]
    for b in batches:
        for j in range(N_PAIRS):
            w = wuv_ref[:, LANES * j:LANES * (j + 1)]
            o0 = _dot(o_lats[b][2 * j * t:(2 * j + 1) * t], w)
            o1 = _dot(o_lats[b][(2 * j + 1) * t:(2 * j + 2) * t], w)
            o_ref[b, :, LANES * j:LANES * (j + 1)] = jnp.where(low, o0, o1).astype(o_ref.dtype)


def _mla_sample_attn(q, cache_ckv, cache_kr128, new_ckv, new_kr128, wuk_t, wuv):
    b, t, _ = q.shape
    n_past = cache_ckv.shape[1]
    q_pos = PAST_LEN + np.arange(t)
    vis_c = _static_visible(q_pos, PAST_LEN - n_past + np.arange(n_past))
    vis_n = _static_visible(q_pos, q_pos)
    visible = None if (vis_c.all() and vis_n.all()) else (vis_c, vis_n)
    g = SAMPLE_GROUP
    blk = lambda a: pl.BlockSpec((g,) + a.shape[1:], lambda i: (i, 0, 0))
    full = lambda a: pl.BlockSpec(a.shape, lambda i: (0, 0))
    return pl.pallas_call(
        functools.partial(_mla_sample_attn_kernel, visible=visible),
        grid=(b // g,),
        in_specs=[blk(q), blk(cache_ckv), blk(cache_kr128), blk(new_ckv), blk(new_kr128),
                  full(wuk_t), full(wuv)],
        out_specs=pl.BlockSpec((g, t, LANES * N_PAIRS), lambda i: (i, 0, 0)),
        out_shape=jax.ShapeDtypeStruct((b, t, LANES * N_PAIRS), BF16),
        compiler_params=_params("parallel"),
        name="mla_sample_attn",
    )(q, cache_ckv, cache_kr128, new_ckv, new_kr128, wuk_t, wuv)


def _out_ln_kernel(op_ref, os_ref, w_ref, *refs, prompt_tiles):
    *x_refs, g_ref, b_ref, y_ref = refs
    x = x_refs[0][...] if len(x_refs) == 1 else _stream_tile(*x_refs, prompt_tiles)
    y = DEEPNORM_ALPHA * x + _dot(_stream_tile(op_ref, os_ref, prompt_tiles), w_ref[...])
    y_ref[...] = _layer_norm(y, g_ref[...], b_ref[...])


def _out_ln(o_p, o_s, w_o, x, g, b):
    tm = TOKEN_TILE
    n_p = o_p.shape[0] // tm
    n = o_p.shape[0] + o_s.shape[0]
    row = pl.BlockSpec((tm, D_MODEL), lambda i: (i, 0))
    full = lambda a: pl.BlockSpec(a.shape, lambda i: (0, 0))
    xs = list(x) if isinstance(x, tuple) else [x]
    x_specs = _stream_specs(tm, D_MODEL, n_p) if isinstance(x, tuple) else [row]
    return pl.pallas_call(
        functools.partial(_out_ln_kernel, prompt_tiles=n_p),
        grid=(n // tm,),
        in_specs=_stream_specs(tm, D_MODEL, n_p) + [full(w_o)] + x_specs + [full(g), full(b)],
        out_specs=row,
        out_shape=jax.ShapeDtypeStruct((n, D_MODEL), F32),
        compiler_params=_params("parallel"),
        name="out_proj_ln",
    )(o_p, o_s, w_o, *xs, g, b)


def _band_qkv_kernel(x_ref, wqk_ref, wv_ref, wvt_ref, q_ref, k_ref, vt_ref, kf_ref, vf_ref, *,
                     first_keep_tile):
    xb = x_ref[...].astype(BF16)
    qk = _dot(xb, wqk_ref[...])
    scale = math.log2(math.e) / math.sqrt(BAND_HEAD_DIM)
    q_ref[...] = (qk[:, :D_MODEL] * scale).astype(BF16)
    k = qk[:, D_MODEL:]
    k_ref[...] = k.astype(BF16)
    vt = _dot_nt(wvt_ref[...], xb).astype(BF16)
    for t in range(vt_ref.shape[0]):
        vt_ref[t] = vt[:, LANES * t:LANES * (t + 1)]

    @pl.when(pl.program_id(0) >= first_keep_tile)
    def _():
        kf_ref[...] = k
        vf_ref[...] = _dot(xb, wv_ref[...])


def _band_qkv(x, w_qkv, n_keep):
    n = x.shape[0]
    tm = TOKEN_TILE
    first = (n - n_keep) // tm
    w = w_qkv.astype(BF16)
    wqk, wv = w[:, :2 * D_MODEL], w[:, 2 * D_MODEL:]
    row = pl.BlockSpec((tm, D_MODEL), lambda i: (i, 0))
    keep = pl.BlockSpec((tm, D_MODEL), lambda i: (jnp.maximum(i - first, 0), 0))
    full = lambda a: pl.BlockSpec(a.shape, lambda i: (0, 0))
    bf = jax.ShapeDtypeStruct((n, D_MODEL), BF16)
    kf = jax.ShapeDtypeStruct((n_keep, D_MODEL), F32)
    return pl.pallas_call(
        functools.partial(_band_qkv_kernel, first_keep_tile=first),
        grid=(n // tm,),
        in_specs=[row, full(wqk), full(wv), full(wv)],
        out_specs=[row, row, pl.BlockSpec((tm // LANES, D_MODEL, LANES), lambda i: (i, 0, 0)),
                   keep, keep],
        out_shape=[bf, bf, jax.ShapeDtypeStruct((n // LANES, D_MODEL, LANES), BF16), kf, kf],
        compiler_params=_params("arbitrary"),
        name="band_qkv",
    )(x, wqk, wv, wv.T)


def _band_prompt_attn_kernel(q_ref, k_ref, vt_ref, rel_ref, o_ref, bias_sc):
    i = pl.program_id(1)
    half = LANES // 2
    low = lax.broadcasted_iota(jnp.int32, (1, LANES), 1) < half
    n_win = BAND_WIN // LANES
    n_left = BAND_KEEP_MAX // LANES

    @pl.when(i == 0)
    def _():
        k_in = lax.broadcasted_iota(jnp.int32, (LANES, LANES), 0)
        q_in = lax.broadcasted_iota(jnp.int32, (LANES, LANES), 1)
        for t in range(n_left + 1):
            q_chunk = (BAND_QSUB * t + q_in) // CHUNK
            for kb in range(n_win):
                k_chunk = (LANES * kb + k_in) // CHUNK
                visible = (k_chunk <= q_chunk) & (k_chunk >= q_chunk - LEFT_CHUNKS)
                start = LANES * (t - kb + n_left)
                for h in range(2):
                    run = jnp.broadcast_to(rel_ref[h, :, start:start + 2 * LANES], (LANES, 2 * LANES))
                    tile = pltpu.roll(run, 0, 1, stride=1, stride_axis=0)[:, LANES:]
                    bias_sc[t, LANES * kb:LANES * (kb + 1), LANES * h:LANES * (h + 1)] = jnp.where(
                        visible, tile, NEG_INF)

    def body(g, c):
        work = []
        for u in range(BAND_GROUP):
            sb = g * BAND_GROUP + u
            blk = i * (BAND_QBLOCK // BAND_QSUB) + sb
            b0 = jnp.maximum(blk - n_left, 0)
            rows = pl.ds(pl.multiple_of(sb * BAND_QSUB, BAND_QSUB), BAND_QSUB)
            q = q_ref[rows, :]
            zero = jnp.zeros_like(q)
            qq = jnp.concatenate([jnp.where(low, q, zero), jnp.where(low, zero, q)], axis=0)
            kw = k_ref[pl.ds(pl.multiple_of(b0 * LANES, LANES), BAND_WIN), :]
            s = _dot_nt(kw, qq) + bias_sc[jnp.minimum(blk, n_left)]
            work.append((rows, b0, s))
        for rows, b0, s in work:
            m = jnp.max(s, 0, keepdims=True)
            p = jnp.exp2(s - m)
            l = jnp.sum(p, 0, keepdims=True)
            pb = p.astype(BF16)
            vt = jnp.concatenate([vt_ref[b0 + w] for w in range(n_win)], axis=1)
            o_t = jnp.concatenate(
                [_dot(vt[half * h:half * (h + 1)], pb[:, LANES * h:LANES * (h + 1)])
                 / l[:, LANES * h:LANES * (h + 1)] for h in range(2)], axis=0)
            o_ref[rows, :] = o_t.T.astype(o_ref.dtype)
        return c

    lax.fori_loop(0, BAND_QBLOCK // BAND_QSUB // BAND_GROUP, body, 0)


def _band_prompt_attn(q, k, vt, rel_table, s):
    n_left = BAND_KEEP_MAX // LANES
    offs = np.arange(LANES * (2 * n_left + 2)) - LANES * (n_left + 1)
    rel_run = (rel_table.astype(F32)[:, np.clip(offs, -REL_MAX, REL_MAX) + REL_MAX]
               * math.log2(math.e))[:, None, :]
    return pl.pallas_call(
        _band_prompt_attn_kernel,
        grid=(N_PAIRS, s // BAND_QBLOCK),
        in_specs=[pl.BlockSpec((BAND_QBLOCK, LANES), lambda j, i: (i, j)),
                  pl.BlockSpec((s, LANES), lambda j, i: (0, j)),
                  pl.BlockSpec((s // LANES, LANES, LANES), lambda j, i: (0, j, 0)),
                  pl.BlockSpec((2, 1, rel_run.shape[-1]), lambda j, i: (j, 0, 0))],
        out_specs=pl.BlockSpec((BAND_QBLOCK, LANES), lambda j, i: (i, j)),
        out_shape=jax.ShapeDtypeStruct((s, D_MODEL), BF16),
        scratch_shapes=[pltpu.VMEM((n_left + 1, BAND_WIN, 2 * LANES), F32)],
        compiler_params=_params("parallel", "arbitrary"),
        name="band_prompt_attn",
    )(q, k, vt, rel_run)


def _band_sample_attn_kernel(q_ref, kc_ref, vc_ref, kn_ref, vn_ref, bc_ref, bn_ref, o_ref):
    lane = lax.broadcasted_iota(jnp.int32, (1, LANES), 1)
    low = lane < LANES // 2
    cols = [slice(LANES * j, LANES * (j + 1)) for j in range(N_PAIRS)]
    scores = []
    for j in range(N_PAIRS):
        q = q_ref[0, :, cols[j]]
        kc = kc_ref[0, :, cols[j]].astype(BF16)
        kn = kn_ref[0, :, cols[j]].astype(BF16)
        for h in range(2):
            qh = jnp.where(low if h == 0 else ~low, q, jnp.zeros_like(q))
            scores.append((_dot_nt(qh, kc) + bc_ref[2 * j + h], _dot_nt(qh, kn) + bn_ref[2 * j + h]))
    probs = []
    for sc, sn in scores:
        m = jnp.maximum(jnp.max(sc, -1, keepdims=True), jnp.max(sn, -1, keepdims=True))
        pc = jnp.exp2(sc - m)
        pn = jnp.exp2(sn - m)
        l = jnp.sum(pc, -1, keepdims=True) + jnp.sum(pn, -1, keepdims=True)
        probs.append((pc.astype(BF16), pn.astype(BF16), l))
    for j in range(N_PAIRS):
        vc = vc_ref[0, :, cols[j]].astype(BF16)
        vn = vn_ref[0, :, cols[j]].astype(BF16)
        outs = [(_dot(pc, vc) + _dot(pn, vn)) / l for pc, pn, l in probs[2 * j:2 * j + 2]]
        o_ref[0, :, cols[j]] = jnp.where(low, outs[0], outs[1]).astype(o_ref.dtype)


def _band_sample_attn(q, kc, vc, kn, vn, bias_c, bias_n):
    b, t, _ = q.shape
    blk = lambda a: pl.BlockSpec((1,) + a.shape[1:], lambda i: (i, 0, 0))
    full = lambda a: pl.BlockSpec(a.shape, lambda i: (0, 0, 0))
    return pl.pallas_call(
        _band_sample_attn_kernel,
        grid=(b,),
        in_specs=[blk(q), blk(kc), blk(vc), blk(kn), blk(vn), full(bias_c), full(bias_n)],
        out_specs=pl.BlockSpec((1, t, D_MODEL), lambda i: (i, 0, 0)),
        out_shape=jax.ShapeDtypeStruct((b, t, D_MODEL), BF16),
        compiler_params=_params("parallel"),
        name="band_sample_attn",
    )(q, kc, vc, kn, vn, bias_c, bias_n)


def _band_bias(rel_table, q_pos, k_pos):
    nq, nk = len(q_pos), len(k_pos)
    assert (np.diff(q_pos) == 1).all() and (np.diff(k_pos) == 1).all()
    span = nq + nk - 1
    diag = int(q_pos[0] - k_pos[0]) + nq - 1 - np.arange(span)
    u = rel_table.astype(F32)[:, np.clip(diag, -REL_MAX, REL_MAX) + REL_MAX] * math.log2(math.e)
    u = jnp.pad(u, ((0, 0), (0, 1)))
    skew = jnp.tile(u, (1, nq))[:, :nq * span].reshape(-1, nq, span)
    bias = skew[:, :, nq - 1:nq - 1 + nk]
    vis = _static_visible(q_pos, k_pos, LEFT_CHUNKS)
    return jnp.where(jnp.asarray(vis)[None], bias, NEG_INF)


def _route(logits_t, rb_ref):
    scores = jax.nn.sigmoid(logits_t)
    rows = [scores[e:e + 1, :] for e in range(N_EXPERTS)]
    biased = [rows[e] + rb_ref[e] for e in range(N_EXPERTS)]
    best = None
    sel = None
    for g in range(N_GROUPS):
        a, b, c, d = biased[EXPERTS_PER_GROUP * g:EXPERTS_PER_GROUP * (g + 1)]
        top2 = jnp.maximum(jnp.maximum(jnp.maximum(a + b, a + c), jnp.maximum(a + d, b + c)),
                           jnp.maximum(b + d, c + d))
        if g == 0:
            best, sel = top2, jnp.zeros(top2.shape, jnp.int32)
        else:
            better = top2 > best
            sel = jnp.where(better, g, sel)
            best = jnp.where(better, top2, best)
    cand = [jnp.where(sel == e // EXPERTS_PER_GROUP, biased[e], NEG_INF) for e in range(N_EXPERTS)]

    def argmax_first(vals):
        top = functools.reduce(jnp.maximum, vals)
        idx = jnp.full(top.shape, N_EXPERTS, jnp.int32)
        for e in reversed(range(N_EXPERTS)):
            idx = jnp.where(vals[e] == top, e, idx)
        return idx

    i1 = argmax_first(cand)
    i2 = argmax_first([jnp.where(i1 == e, -jnp.inf, cand[e]) for e in range(N_EXPERTS)])
    picked = [jnp.where((i1 == e) | (i2 == e), rows[e], 0.0) for e in range(N_EXPERTS)]
    total = functools.reduce(jnp.add, picked)
    return [p / total for p in picked]


def _moe_kernel(x_ref, rw_ref, rb_ref, wg_ref, wu_ref, wd_ref, g_ref, b_ref, *refs, prompt_tiles):
    *y_refs, xb_sc, comb_sc, acc_sc = refs
    grp = pl.program_id(1)

    def hidden(xb):
        return [jax.nn.silu(_dot(xb, wg_ref[e])) * _dot(xb, wu_ref[e])
                for e in range(EXPERTS_PER_GROUP)]

    def combine(hs, comb):
        parts = [(h * comb[:, e:e + 1]).astype(BF16) for e, h in enumerate(hs)]
        return _dot(jnp.concatenate(parts, axis=1), wd_ref[0])

    @pl.when(grp == 0)
    def _():
        x = x_ref[...]
        xb = x.astype(BF16)
        xb_sc[...] = xb
        hs = hidden(xb)
        x_lo = (x - xb.astype(F32)).astype(BF16)
        rw_hi = rw_ref[0]
        logits_t = _dot_nt(rw_hi, xb) + (_dot_nt(rw_hi, x_lo) + _dot_nt(rw_ref[1], xb))
        comb_rows = _route(logits_t, rb_ref)
        comb = jnp.concatenate(comb_rows, axis=0).T
        for g in range(N_GROUPS):
            comb_sc[g] = comb[:, EXPERTS_PER_GROUP * g:EXPERTS_PER_GROUP * (g + 1)]
        acc_sc[...] = DEEPNORM_ALPHA * x + combine(hs, comb[:, :EXPERTS_PER_GROUP])

    @pl.when(grp > 0)
    def _():
        acc_sc[...] += combine(hidden(xb_sc[...]), comb_sc[grp])

    @pl.when(grp == N_GROUPS - 1)
    def _():
        y = _layer_norm(acc_sc[...], g_ref[...], b_ref[...])
        if len(y_refs) == 1:
            y_refs[0][...] = y
        else:
            is_prompt = pl.program_id(0) < prompt_tiles

            @pl.when(is_prompt)
            def _():
                y_refs[0][...] = y

            @pl.when(jnp.logical_not(is_prompt))
            def _():
                y_refs[1][...] = y


def _moe_ln(x, rw_t, rb, wg, wu, wd, layer, g, b, n_prompt=None):
    n = x.shape[0]
    tm = MOE_TILE
    row = pl.BlockSpec((tm, D_MODEL), lambda i, j: (i, 0))
    full = lambda a: pl.BlockSpec(a.shape, lambda i, j: (0, 0))
    grp = lambda a: pl.BlockSpec((None, a.shape[1] // N_GROUPS) + a.shape[2:],
                                 lambda i, j: (layer, j, 0, 0))
    if n_prompt is None:
        n_p = n // tm
        out_specs = row
        out_shape = jax.ShapeDtypeStruct((n, D_MODEL), F32)
        semantics = ("parallel", "arbitrary")
    else:
        n_p = n_prompt // tm
        out_specs = _stream_specs(tm, D_MODEL, n_p)
        out_shape = [jax.ShapeDtypeStruct((n_prompt, D_MODEL), F32),
                     jax.ShapeDtypeStruct((n - n_prompt, D_MODEL), F32)]
        semantics = ("arbitrary", "arbitrary")
    return pl.pallas_call(
        functools.partial(_moe_kernel, prompt_tiles=n_p),
        grid=(n // tm, N_GROUPS),
        in_specs=[row, pl.BlockSpec(rw_t.shape, lambda i, j: (0, 0, 0)),
                  pl.BlockSpec(memory_space=pltpu.SMEM),
                  grp(wg), grp(wu), grp(wd), full(g), full(b)],
        out_specs=out_specs,
        out_shape=out_shape,
        scratch_shapes=[pltpu.VMEM((tm, D_MODEL), BF16),
                        pltpu.VMEM((N_GROUPS, tm, EXPERTS_PER_GROUP), F32),
                        pltpu.VMEM((tm, D_MODEL), F32)],
        compiler_params=_params(*semantics),
        name="moe_ln",
    )(x, rw_t, rb, wg, wu, wd, g, b)


def _rope_tables():
    half = MLA_ROPE // 2
    inv = ROPE_THETA ** (-jnp.arange(half, dtype=F32) / half)
    pos = jnp.concatenate([jnp.arange(SEQ, dtype=jnp.int32),
                           jnp.tile(PAST_LEN + jnp.arange(DEC_SEQ, dtype=jnp.int32), DEC_BATCH)])
    ang = pos.astype(F32)[:, None] * inv[None, :]
    reps = LANES // half
    return jnp.tile(jnp.cos(ang), (1, reps)), jnp.tile(jnp.sin(ang), (1, reps))


def _rot_cols(w):
    half = MLA_ROPE // 2
    return jnp.concatenate([-w[..., half:], w[..., :half]], -1)


def _mla_weights(w_in, w_uq, w_uk, w_uv, w_o):
    o_kv = MLA_Q_LORA + MLA_KV_LORA
    w_r = w_in[:, o_kv:]
    zpad = jnp.zeros((D_MODEL, LANES - 2 * MLA_ROPE), F32)
    w_in_p = jnp.concatenate([w_in[:, :o_kv], w_r, w_r, zpad,
                              _rot_cols(w_r), _rot_cols(w_r), zpad], 1).astype(BF16)
    wq = w_uq.reshape(MLA_Q_LORA, N_PAIRS, 2, MLA_QK)
    nope = wq[..., :MLA_NOPE].reshape(MLA_Q_LORA, N_PAIRS, 2 * MLA_NOPE)
    ropew = wq[..., MLA_NOPE:]
    zq = jnp.zeros((MLA_Q_LORA, N_PAIRS, LANES - 2 * MLA_ROPE), F32)
    rope_cols = jnp.concatenate([ropew.reshape(MLA_Q_LORA, N_PAIRS, 2 * MLA_ROPE), zq], -1)
    rot_cols = jnp.concatenate([_rot_cols(ropew).reshape(MLA_Q_LORA, N_PAIRS, 2 * MLA_ROPE), zq], -1)
    wqa = jnp.concatenate([nope, rope_cols], -1).reshape(MLA_Q_LORA, 2 * LANES * N_PAIRS).astype(BF16)
    wqb = rot_cols.reshape(MLA_Q_LORA, LANES * N_PAIRS).astype(BF16)
    return w_in_p, wqa, wqb, w_uk.astype(BF16), w_uv.astype(BF16), w_o.astype(BF16)


def _group_experts(w_gate, w_up, w_down):
    wd = w_down.reshape(DEPTH, N_GROUPS, EXPERTS_PER_GROUP * D_EXPERT, D_MODEL).astype(BF16)
    return w_gate.astype(BF16), w_up.astype(BF16), wd


def kernel(x_prompt, x_sample, cache_mla_ckv, cache_mla_krope, cache_band_k, cache_band_v,
           mla_w_in, mla_q_norm, mla_kv_norm, mla_w_uq, mla_w_uk, mla_w_uv, mla_w_o,
           band_w_qkv, band_rel_bias, band_w_o,
           router_w, router_b, moe_w_gate, moe_w_up, moe_w_down,
           ln_mix_g, ln_mix_b, ln_ffn_g, ln_ffn_b):
    x = (x_prompt.reshape(SEQ, D_MODEL), x_sample.reshape(N_SAMPLE, D_MODEL))
    wg, wu, wd = _group_experts(moe_w_gate, moe_w_up, moe_w_down)
    cos, sin = _rope_tables()
    rw_f = router_w.T.astype(F32)
    rw_hi = rw_f.astype(BF16)
    rw_t = jnp.stack([rw_hi, (rw_f - rw_hi.astype(F32)).astype(BF16)], 0)
    rb = router_b.astype(F32)
    band_keep = cache_band_k.shape[2]
    row2 = lambda a: a.reshape(1, -1)

    ckv_p, kr_p, ckv_s, kr_s = [], [], [], []
    bk_p, bv_p, bk_s, bv_s = [], [], [], []
    for i in range(DEPTH):
        j = i // N_MIXERS
        if i % N_MIXERS == 0:
            w_in_p, wqa, wqb, wuk, wuv, w_o = _mla_weights(
                mla_w_in[j], mla_w_uq[j], mla_w_uk[j], mla_w_uv[j], mla_w_o[j])
            x_pair = x if isinstance(x, tuple) else (x[:SEQ], x[SEQ:])
            ckv, kr128, q = _mla_proj(*x_pair, w_in_p, row2(mla_q_norm[j]), row2(mla_kv_norm[j]),
                                      wqa, wqb, cos, sin)
            kr = kr128[:, :MLA_ROPE]
            ckv_p.append(ckv[:SEQ].reshape(1, SEQ, MLA_KV_LORA))
            kr_p.append(kr[:SEQ].reshape(1, SEQ, MLA_ROPE))
            ckv_s.append(ckv[SEQ:].reshape(DEC_BATCH, DEC_SEQ, MLA_KV_LORA))
            kr_s.append(kr[SEQ:].reshape(DEC_BATCH, DEC_SEQ, MLA_ROPE))
            k_p, vt_p = _kv_expand(ckv, kr128, wuk, wuv.T, SEQ)
            o_p = _mla_prompt_attn(q, k_p, vt_p)
            cache_kr = cache_mla_krope[j].astype(BF16)
            cache_kr128 = jnp.concatenate(
                [cache_kr, cache_kr, jnp.zeros(cache_kr.shape[:2] + (LANES - 2 * MLA_ROPE,), BF16)], -1)
            o_s = _mla_sample_attn(q[SEQ:].reshape(DEC_BATCH, DEC_SEQ, -1), cache_mla_ckv[j], cache_kr128,
                                   ckv_s[-1], kr128[SEQ:].reshape(DEC_BATCH, DEC_SEQ, LANES), wuk.T, wuv)
        else:
            n_keep = min(BAND_KEEP_MAX, SEQ) + N_SAMPLE
            x_all = jnp.concatenate(x, 0) if isinstance(x, tuple) else x
            q, k, vt, k_f, v_f = _band_qkv(x_all, band_w_qkv[j], n_keep)
            keep = n_keep - N_SAMPLE
            hd = (BAND_HEADS, BAND_HEAD_DIM)
            bk_p.append(k_f[:keep].reshape((1, keep) + hd))
            bv_p.append(v_f[:keep].reshape((1, keep) + hd))
            bk_s.append(k_f[keep:].reshape((DEC_BATCH, DEC_SEQ) + hd))
            bv_s.append(v_f[keep:].reshape((DEC_BATCH, DEC_SEQ) + hd))
            w_o = band_w_o[j].astype(BF16)
            o_p = _band_prompt_attn(q, k, vt, band_rel_bias[j], SEQ)
            q_pos = PAST_LEN + np.arange(DEC_SEQ)
            pos_c = PAST_LEN - band_keep + np.arange(band_keep)
            bias_c = _band_bias(band_rel_bias[j], q_pos, pos_c)
            bias_n = _band_bias(band_rel_bias[j], q_pos, q_pos)
            s3 = lambda a: a.reshape(DEC_BATCH, DEC_SEQ, D_MODEL)
            cache3 = lambda a: a.reshape(DEC_BATCH, band_keep, D_MODEL)
            o_s = _band_sample_attn(s3(q[SEQ:]), cache3(cache_band_k[j]), cache3(cache_band_v[j]),
                                    s3(k_f[keep:]), s3(v_f[keep:]), bias_c, bias_n)
        x = _out_ln(o_p, o_s.reshape(N_SAMPLE, D_MODEL), w_o, x, row2(ln_mix_g[i]), row2(ln_mix_b[i]))
        x = _moe_ln(x, rw_t, rb, wg, wu, wd, i, row2(ln_ffn_g[i]), row2(ln_ffn_b[i]),
                    n_prompt=SEQ if i == DEPTH - 1 else None)

    return (x[0].reshape(1, SEQ, D_MODEL), x[1].reshape(DEC_BATCH, DEC_SEQ, D_MODEL),
            jnp.stack(ckv_p, 0), jnp.stack(kr_p, 0), jnp.stack(ckv_s, 0), jnp.stack(kr_s, 0),
            jnp.stack(bk_p, 0), jnp.stack(bv_p, 0), jnp.stack(bk_s, 0), jnp.stack(bv_s, 0))
```
